```python
import jax, jax.numpy as jnp
from jax import lax
import numpy as np

D_MODEL = 2048
BATCH = 8
SEQ = 2048
DEPTH = 2

CTX_LEN = 256
GRID_W = 64
HEAD_DIM = 128
N_Q_HEADS = D_MODEL // HEAD_DIM
N_KV_HEADS = N_Q_HEADS // 4
GQA_GROUP = N_Q_HEADS // N_KV_HEADS
D_Q = N_Q_HEADS * HEAD_DIM
D_KV = N_KV_HEADS * HEAD_DIM
D_CONV = D_MODEL
CONV_WIDTH = 3
N_EXPERTS = 32
TOP_K = 4
D_EXPERT = D_MODEL // 2
SWIGLU_ALPHA = 1.702
SWIGLU_LIMIT = 7.0
ROPE_THETA = 10000.0
ROPE_AXIS_DIM = HEAD_DIM // 2
Q_BLOCK = 128
EXPERT_BLOCK = 128
NORM_EPS = 1e-6
DEEPNORM_ALPHA = (2 * DEPTH) ** 0.25
DEEPNORM_BETA = (8 * DEPTH) ** -0.25

O_Q = 0
O_K = O_Q + D_Q
O_V = O_K + D_KV
O_CB = O_V + D_KV
O_CC = O_CB + D_CONV
O_CX = O_CC + D_CONV
O_GA = O_CX + D_CONV
O_GC = O_GA + D_MODEL
N_IN = O_GC + D_MODEL

kernel_name = "hybrid_gqa_shortconv_moe_dit_block"


def layer_norm(x, g, b):
    xf = x.astype(jnp.float32)
    mu = jnp.mean(xf, axis=-1, keepdims=True)
    var = jnp.mean(jnp.square(xf - mu), axis=-1, keepdims=True)
    y = (xf - mu) * lax.rsqrt(var + NORM_EPS)
    return (y * g.astype(jnp.float32) + b.astype(jnp.float32)).astype(x.dtype)


def rms_norm(x, g):
    xf = x.astype(jnp.float32)
    y = xf * lax.rsqrt(jnp.mean(jnp.square(xf), axis=-1, keepdims=True) + NORM_EPS)
    return (y * g.astype(jnp.float32)).astype(x.dtype)


def axial_rope_tables(n_tok):
    rows = n_tok // GRID_W
    row = jnp.repeat(jnp.arange(rows, dtype=jnp.int32), GRID_W).astype(jnp.float32)
    col = jnp.tile(jnp.arange(GRID_W, dtype=jnp.int32), rows).astype(jnp.float32)
    inv = ROPE_THETA ** (-jnp.arange(0, ROPE_AXIS_DIM, 2, dtype=jnp.float32) / ROPE_AXIS_DIM)
    ang = jnp.concatenate([row[:, None] * inv, col[:, None] * inv], axis=-1)
    return jnp.cos(ang), jnp.sin(ang)


def apply_rope(x, cos, sin):
    xf = x.astype(jnp.float32).reshape(*x.shape[:-1], HEAD_DIM // 2, 2)
    x0, x1 = xf[..., 0], xf[..., 1]
    c = cos[None, :, None, :]
    s = sin[None, :, None, :]
    out = jnp.stack([x0 * c - x1 * s, x0 * s + x1 * c], axis=-1).reshape(x.shape)
    return out.astype(x.dtype)


def attend(qb, k, v):
    s = jnp.einsum('bqkgd,bskd->bkgqs', qb, k, preferred_element_type=jnp.float32)
    p = jax.nn.softmax(s, axis=-1)
    return jnp.einsum('bkgqs,bskd->bqkgd', p.astype(v.dtype), v)


def latent_attention(q, k_all, v_all):
    B, S = q.shape[0], q.shape[1]
    n_blk = S // Q_BLOCK
    qb = q.reshape(B, n_blk, Q_BLOCK, N_KV_HEADS, GQA_GROUP, HEAD_DIM).transpose(1, 0, 2, 3, 4, 5)
    o = lax.map(lambda qi: attend(qi, k_all, v_all), qb)
    return o.transpose(1, 0, 2, 3, 4, 5).reshape(B, S, D_Q)


def short_conv_mixer(gb, gc, xin, conv_w):
    z = gc * xin
    zp = jnp.pad(z, ((0, 0), (1, 1), (0, 0)))
    zc = conv_w[0] * zp[:, :-2] + conv_w[1] * zp[:, 1:-1] + conv_w[2] * zp[:, 2:]
    return gb * zc


def merge_branches(p, attn, conv_w, w_attn_o, w_conv_o, w_mix_o):
    conv = short_conv_mixer(p[..., O_CB:O_CC], p[..., O_CC:O_CX], p[..., O_CX:O_GA], conv_w)
    y = (jax.nn.sigmoid(p[..., O_GA:O_GC]) * jnp.einsum('bse,ed->bsd', attn, w_attn_o)
         + jax.nn.sigmoid(p[..., O_GC:N_IN]) * jnp.einsum('bse,ed->bsd', conv, w_conv_o))
    return jnp.einsum('bsd,de->bse', y, w_mix_o)


def hybrid_mixer(u_l, u_c, w_in, q_g, k_g, conv_w, w_attn_o, w_conv_o, w_mix_o, cos, sin, ctx_out):
    B, S = u_l.shape[0], u_l.shape[1]
    L = u_c.shape[1]
    scale = HEAD_DIM ** -0.5
    p = jnp.einsum('bsd,de->bse', u_l, w_in)
    q = rms_norm(p[..., O_Q:O_K].reshape(B, S, N_Q_HEADS, HEAD_DIM), q_g)
    k = rms_norm(p[..., O_K:O_V].reshape(B, S, N_KV_HEADS, HEAD_DIM), k_g)
    v = p[..., O_V:O_CB].reshape(B, S, N_KV_HEADS, HEAD_DIM)
    q = apply_rope(q, cos, sin) * scale
    k = apply_rope(k, cos, sin)
    if ctx_out:
        pc = jnp.einsum('bld,de->ble', u_c, w_in)
        kvc = pc[..., O_K:O_CB]
    else:
        kvc = jnp.einsum('bld,de->ble', u_c, w_in[:, O_K:O_CB])
    kc = rms_norm(kvc[..., :D_KV].reshape(B, L, N_KV_HEADS, HEAD_DIM), k_g)
    vc = kvc[..., D_KV:].reshape(B, L, N_KV_HEADS, HEAD_DIM)
    k_all = jnp.concatenate([kc, k], axis=1)
    v_all = jnp.concatenate([vc, v], axis=1)
    attn_l = latent_attention(q, k_all, v_all)
    out_l = merge_branches(p, attn_l, conv_w, w_attn_o, w_conv_o, w_mix_o)
    if not ctx_out:
        return out_l, None
    qc = rms_norm(pc[..., O_Q:O_K].reshape(B, L, N_Q_HEADS, HEAD_DIM), q_g) * scale
    attn_c = attend(qc.reshape(B, L, N_KV_HEADS, GQA_GROUP, HEAD_DIM), kc, vc).reshape(B, L, D_Q)
    out_c = merge_branches(pc, attn_c, conv_w, w_attn_o, w_conv_o, w_mix_o)
    return out_l, out_c


def moe_ffn(v, router_w, router_b, w_up, b_up, w_down, b_down):
    T, D = v.shape
    logits = jnp.einsum('td,de->te', v, router_w, preferred_element_type=jnp.float32) + router_b.astype(jnp.float32)
    top_logit, top_idx = lax.top_k(logits, TOP_K)
    gate = jax.nn.softmax(top_logit, axis=-1)
    n_assign = T * TOP_K
    flat_e = top_idx.reshape(-1)
    flat_g = gate.reshape(-1)
    flat_t = jnp.arange(n_assign, dtype=jnp.int32) // TOP_K
    order = jnp.argsort(flat_e)
    se = flat_e[order]
    counts = jnp.bincount(flat_e, length=N_EXPERTS)
    padded = (counts + EXPERT_BLOCK - 1) // EXPERT_BLOCK * EXPERT_BLOCK
    pend = jnp.cumsum(padded)
    pstart = pend - padded
    ustart = jnp.cumsum(counts) - counts
    dest = pstart[se] + jnp.arange(n_assign, dtype=jnp.int32) - ustart[se]
    n_blocks = -(-n_assign // EXPERT_BLOCK) + N_EXPERTS
    n_slots = n_blocks * EXPERT_BLOCK
    slot_tok = jnp.zeros((n_slots,), jnp.int32).at[dest].set(flat_t[order])
    slot_gate = jnp.zeros((n_slots,), jnp.float32).at[dest].set(flat_g[order])
    block_e = jnp.minimum(
        jnp.searchsorted(pend, jnp.arange(n_blocks, dtype=jnp.int32) * EXPERT_BLOCK, side='right'),
        N_EXPERTS - 1)

    def expert_block(args):
        e, tok, g = args
        xb = v[tok]
        hb = xb @ w_up[e] + b_up[e]
        glu = jnp.minimum(hb[:, :D_EXPERT], SWIGLU_LIMIT)
        lin = jnp.clip(hb[:, D_EXPERT:], -SWIGLU_LIMIT, SWIGLU_LIMIT)
        act = glu * jax.nn.sigmoid(SWIGLU_ALPHA * glu) * (lin + 1.0)
        return (act @ w_down[e] + b_down[e]) * g[:, None].astype(v.dtype)

    out = lax.map(expert_block, (block_e,
                                 slot_tok.reshape(n_blocks, EXPERT_BLOCK),
                                 slot_gate.reshape(n_blocks, EXPERT_BLOCK)))
    return jax.ops.segment_sum(out.reshape(n_slots, D), slot_tok, num_segments=T)


def setup_inputs(seed: int = 0) -> dict:
    key = jax.random.key(seed)
    ks = jax.random.split(key, 24)
    f32 = jnp.float32

    def nrm(k, shape, s):
        return jax.random.normal(k, shape, f32) * s

    D = D_MODEL
    return {
        "x": nrm(ks[0], (BATCH, SEQ, D), 1.0),
        "c": nrm(ks[1], (BATCH, D), 1.0),
        "ctx": nrm(ks[2], (BATCH, CTX_LEN, D), 1.0),
        "c_ctx": nrm(ks[3], (D,), 1.0),
        "ada_w": nrm(ks[4], (DEPTH, D, 6 * D), D ** -0.5),
        "ada_b": nrm(ks[5], (DEPTH, 6 * D), 0.02),
        "w_in": nrm(ks[6], (DEPTH, D, N_IN), D ** -0.5),
        "q_norm_g": 1.0 + nrm(ks[7], (DEPTH, HEAD_DIM), 0.02),
        "k_norm_g": 1.0 + nrm(ks[8], (DEPTH, HEAD_DIM), 0.02),
        "conv_w": nrm(ks[9], (DEPTH, CONV_WIDTH, D_CONV), CONV_WIDTH ** -0.5),
        "w_attn_o": nrm(ks[10], (DEPTH, D_Q, D), D_Q ** -0.5),
        "w_conv_o": nrm(ks[11], (DEPTH, D_CONV, D), D_CONV ** -0.5),
        "w_mix_o": nrm(ks[12], (DEPTH, D, D), DEEPNORM_BETA * D ** -0.5),
        "ln1_g": 1.0 + nrm(ks[13], (DEPTH, D), 0.02),
        "ln1_b": nrm(ks[14], (DEPTH, D), 0.02),
        "router_w": nrm(ks[15], (DEPTH, D, N_EXPERTS), D ** -0.5),
        "router_b": nrm(ks[16], (DEPTH, N_EXPERTS), 0.01),
        "w_up": nrm(ks[17], (DEPTH, N_EXPERTS, D, 2 * D_EXPERT), D ** -0.5),
        "b_up": nrm(ks[18], (DEPTH, N_EXPERTS, 2 * D_EXPERT), 0.02),
        "w_down": nrm(ks[19], (DEPTH, N_EXPERTS, D_EXPERT, D), DEEPNORM_BETA * D_EXPERT ** -0.5),
        "b_down": nrm(ks[20], (DEPTH, N_EXPERTS, D), 0.02),
        "ln2_g": 1.0 + nrm(ks[21], (DEPTH, D), 0.02),
        "ln2_b": nrm(ks[22], (DEPTH, D), 0.02),
    }


def reference(x, c, ctx, c_ctx, ada_w, ada_b, w_in, q_norm_g, k_norm_g, conv_w, w_attn_o, w_conv_o,
              w_mix_o, ln1_g, ln1_b, router_w, router_b, w_up, b_up, w_down, b_down, ln2_g, ln2_b):
    B, S, D = x.shape
    L = ctx.shape[1]
    cos, sin = axial_rope_tables(S)
    h = ctx
    silu_c = jax.nn.silu(c)
    silu_cc = jax.nn.silu(c_ctx)
    for i in range(DEPTH):
        last = i == DEPTH - 1
        mod = jnp.einsum('bd,de->be', silu_c, ada_w[i]) + ada_b[i]
        sh1, sc1, g1, sh2, sc2, g2 = [m[:, None, :] for m in jnp.split(mod, 6, axis=-1)]
        modc = silu_cc @ ada_w[i] + ada_b[i]
        sh1c, sc1c, g1c, sh2c, sc2c, g2c = jnp.split(modc, 6)

        u_l = x * (1.0 + sc1) + sh1
        u_c = h * (1.0 + sc1c) + sh1c
        mix_l, mix_c = hybrid_mixer(u_l, u_c, w_in[i], q_norm_g[i], k_norm_g[i], conv_w[i],
                                    w_attn_o[i], w_conv_o[i], w_mix_o[i], cos, sin, not last)
        x = layer_norm(DEEPNORM_ALPHA * x + g1 * mix_l, ln1_g[i], ln1_b[i])

        v_l = x * (1.0 + sc2) + sh2
        if last:
            f_l = moe_ffn(v_l.reshape(B * S, D), router_w[i], router_b[i], w_up[i], b_up[i],
                          w_down[i], b_down[i]).reshape(B, S, D)
        else:
            h = layer_norm(DEEPNORM_ALPHA * h + g1c * mix_c, ln1_g[i], ln1_b[i])
            v_c = h * (1.0 + sc2c) + sh2c
            tokens = jnp.concatenate([v_l.reshape(B * S, D), v_c.reshape(B * L, D)], axis=0)
            f = moe_ffn(tokens, router_w[i], router_b[i], w_up[i], b_up[i], w_down[i], b_down[i])
            f_l = f[:B * S].reshape(B, S, D)
            f_c = f[B * S:].reshape(B, L, D)
            h = layer_norm(DEEPNORM_ALPHA * h + g2c * f_c, ln2_g[i], ln2_b[i])
        x = layer_norm(DEEPNORM_ALPHA * x + g2 * f_l, ln2_g[i], ln2_b[i])
    return x
```

```python
import functools
import math

import jax
import jax.numpy as jnp
from jax import lax
from jax.experimental import pallas as pl
from jax.experimental.pallas import tpu as pltpu

F32 = jnp.float32
BF16 = jnp.bfloat16
I32 = jnp.int32

LANES = 128
HEAD_DIM = 128
GQA_GROUP = 4
GRID_W = 64
TOP_K = 4
SWIGLU_ALPHA = 1.702
SWIGLU_LIMIT = 7.0
ROPE_THETA = 10000.0
NORM_EPS = 1e-6
VMEM_LIMIT = 56 * 1024 * 1024
EXPERT_ROWS = 256
META_IDX, META_GATE, META_POS = 0, 4, 8


def _params(sem, vmem=VMEM_LIMIT):
    return pltpu.CompilerParams(dimension_semantics=sem, vmem_limit_bytes=vmem)


def _tile(n, pref):
    if n <= pref:
        return n
    t = pref - pref % 8
    while n % t:
        t -= 8
    return t


def _adaln_kernel(c_ref, w_ref, b_ref, o_ref):
    c = c_ref[...]
    s = (c * jax.nn.sigmoid(c)).astype(BF16)
    o_ref[...] = jnp.dot(s, w_ref[...].astype(BF16), preferred_element_type=F32) + b_ref[...]


def _adaln(cvec, ada_w, ada_b):
    depth, d, n = ada_w.shape
    r = cvec.shape[0]
    tn = _tile(n, 1024)
    return pl.pallas_call(
        _adaln_kernel,
        grid=(depth, n // tn),
        in_specs=[
            pl.BlockSpec((r, d), lambda l, j: (0, 0)),
            pl.BlockSpec((None, d, tn), lambda l, j: (l, 0, j)),
            pl.BlockSpec((None, 1, tn), lambda l, j: (l, 0, j)),
        ],
        out_specs=pl.BlockSpec((None, r, tn), lambda l, j: (l, 0, j)),
        out_shape=jax.ShapeDtypeStruct((depth, r, n), F32),
        compiler_params=_params(("parallel", "parallel")),
        name="adaln",
    )(cvec, ada_w, ada_b.reshape(depth, 1, n))


def _modulate(u_scr, x_ref, sc_ref, sh_ref):
    @pl.when(pl.program_id(1) == 0)
    def _():
        u_scr[...] = (x_ref[...] * (1.0 + sc_ref[...]) + sh_ref[...]).astype(BF16)


def _proj_qk_kernel(x_ref, sc_ref, sh_ref, w_ref, g_ref, cos_ref, sin_ref, o_ref, u_scr):
    _modulate(u_scr, x_ref, sc_ref, sh_ref)
    acc = jnp.dot(u_scr[...], w_ref[...], preferred_element_type=F32)
    tm, tn = acc.shape
    cos = cos_ref[...]
    sin = sin_ref[...]
    lane = lax.broadcasted_iota(I32, (tm, HEAD_DIM), 1)
    even = (lane & 1) == 0
    for h in range(tn // HEAD_DIM):
        cols = slice(h * HEAD_DIM, (h + 1) * HEAD_DIM)
        a = acc[:, cols]
        ms = jnp.mean(a * a, axis=-1, keepdims=True)
        y = a * lax.rsqrt(ms + NORM_EPS) * g_ref[:, cols]
        partner = jnp.where(even, pltpu.roll(y, HEAD_DIM - 1, 1), pltpu.roll(y, 1, 1))
        o_ref[:, cols] = (y * cos + partner * sin).astype(BF16)


def _proj_plain_kernel(x_ref, sc_ref, sh_ref, w_ref, o_ref, u_scr):
    _modulate(u_scr, x_ref, sc_ref, sh_ref)
    o_ref[...] = jnp.dot(u_scr[...], w_ref[...], preferred_element_type=F32).astype(BF16)


def _proj_sigmoid_kernel(x_ref, sc_ref, sh_ref, w_ref, o_ref, u_scr):
    _modulate(u_scr, x_ref, sc_ref, sh_ref)
    acc = jnp.dot(u_scr[...], w_ref[...], preferred_element_type=F32)
    o_ref[...] = jax.nn.sigmoid(acc).astype(BF16)


def _proj_product_kernel(x_ref, sc_ref, sh_ref, wc_ref, wx_ref, o_ref, u_scr):
    _modulate(u_scr, x_ref, sc_ref, sh_ref)
    u = u_scr[...]
    c = jnp.dot(u, wc_ref[...], preferred_element_type=F32)
    xin = jnp.dot(u, wx_ref[...], preferred_element_type=F32)
    o_ref[...] = (c * xin).astype(BF16)


class _Rows:
    def __init__(self, n_rows, tm, batch, seq, n_lat):
        assert n_rows % tm == 0 and seq % tm == 0 and n_lat % tm == 0
        self.n_rows, self.tm, self.batch = n_rows, tm, batch
        self.n_tiles = n_rows // tm
        self.lat_tiles = n_lat // tm
        self.per_seq = seq // tm

    def mod_row(self, i):
        return jnp.where(i < self.lat_tiles, i // self.per_seq, self.batch)

    def rope_blk(self, i):
        return jnp.where(i < self.lat_tiles, i % self.per_seq, self.per_seq)


def _proj(kind, rows, x, sc, sh, w, col_offs, n_cols, tn, extra=(), out_map=None, name="proj"):
    d = x.shape[1]
    tm = rows.tm
    n_j = n_cols // tn
    kernels = {"qk": _proj_qk_kernel, "plain": _proj_plain_kernel,
               "sigmoid": _proj_sigmoid_kernel, "product": _proj_product_kernel}
    in_specs = [
        pl.BlockSpec((tm, d), lambda i, j: (i, 0)),
        pl.BlockSpec((None, 1, d), lambda i, j: (rows.mod_row(i), 0, 0)),
        pl.BlockSpec((None, 1, d), lambda i, j: (rows.mod_row(i), 0, 0)),
    ]
    operands = [x, sc, sh]
    for off in col_offs:
        assert off % tn == 0
        in_specs.append(pl.BlockSpec((d, tn), functools.partial(lambda i, j, o: (0, o + j), o=off // tn)))
        operands.append(w)
    if kind == "qk":
        gains, cos, sin = extra
        in_specs += [
            pl.BlockSpec((1, tn), lambda i, j: (0, j)),
            pl.BlockSpec((tm, HEAD_DIM), lambda i, j: (rows.rope_blk(i), 0)),
            pl.BlockSpec((tm, HEAD_DIM), lambda i, j: (rows.rope_blk(i), 0)),
        ]
        operands += [gains, cos, sin]
    if out_map is None:
        out_map = lambda i, j: (i, j)
    return pl.pallas_call(
        kernels[kind],
        grid=(rows.n_tiles, n_j),
        in_specs=in_specs,
        out_specs=pl.BlockSpec((tm, tn), out_map),
        out_shape=jax.ShapeDtypeStruct((rows.n_rows, n_cols), BF16),
        scratch_shapes=[pltpu.VMEM((tm, d), BF16)],
        compiler_params=_params(("parallel", "arbitrary")),
        name=name,
    )(*operands)


def _attn_heads(q_ref, kl_ref, vl_ref, kc_ref, vc_ref, o_ref, with_latent):
    nt = (((1,), (1,)), ((), ()))
    for h in range(GQA_GROUP):
        cols = slice(h * HEAD_DIM, (h + 1) * HEAD_DIM)
        q = q_ref[:, cols]
        s_c = lax.dot_general(q, kc_ref[...], nt, preferred_element_type=F32)
        m = jnp.max(s_c, axis=-1, keepdims=True)
        if with_latent:
            s_l = lax.dot_general(q, kl_ref[...], nt, preferred_element_type=F32)
            m = jnp.maximum(m, jnp.max(s_l, axis=-1, keepdims=True))
        p_c = jnp.exp(s_c - m)
        den = jnp.sum(p_c, axis=-1, keepdims=True)
        o = jnp.dot(p_c.astype(BF16), vc_ref[...], preferred_element_type=F32)
        if with_latent:
            p_l = jnp.exp(s_l - m)
            den = den + jnp.sum(p_l, axis=-1, keepdims=True)
            o = o + jnp.dot(p_l.astype(BF16), vl_ref[...], preferred_element_type=F32)
        o_ref[:, cols] = (o / den).astype(BF16)


def _attn_kernel(q_ref, kl_ref, vl_ref, kc_ref, vc_ref, o_ref, *, lat_steps, ctx_step):
    if not ctx_step:
        _attn_heads(q_ref, kl_ref, vl_ref, kc_ref, vc_ref, o_ref, True)
        return
    qi = pl.program_id(2)

    @pl.when(qi < lat_steps)
    def _():
        _attn_heads(q_ref, kl_ref, vl_ref, kc_ref, vc_ref, o_ref, True)

    @pl.when(qi >= lat_steps)
    def _():
        _attn_heads(q_ref, kl_ref, vl_ref, kc_ref, vc_ref, o_ref, False)


def _attention(qk, vcb, batch, seq, ctx_len, d_q, d_conv, ctx_queries):
    n_kv = d_q // HEAD_DIM // GQA_GROUP
    tq = ctx_len
    lat_steps = seq // tq
    n_lat_blk = batch * seq // tq
    gw = GQA_GROUP * HEAD_DIM
    k_col = d_q // HEAD_DIM
    v_col = d_conv // HEAD_DIM
    n_rows = batch * seq + (batch * ctx_len if ctx_queries else 0)

    def q_map(b, g, qi):
        return (jnp.where(qi < lat_steps, b * lat_steps + qi, n_lat_blk + b), g)

    return pl.pallas_call(
        functools.partial(_attn_kernel, lat_steps=lat_steps, ctx_step=ctx_queries),
        grid=(batch, n_kv, lat_steps + (1 if ctx_queries else 0)),
        in_specs=[
            pl.BlockSpec((tq, gw), q_map),
            pl.BlockSpec((seq, HEAD_DIM), lambda b, g, qi: (b, k_col + g)),
            pl.BlockSpec((seq, HEAD_DIM), lambda b, g, qi: (b, v_col + g)),
            pl.BlockSpec((ctx_len, HEAD_DIM), lambda b, g, qi: (n_lat_blk + b, k_col + g)),
            pl.BlockSpec((ctx_len, HEAD_DIM), lambda b, g, qi: (n_lat_blk + b, v_col + g)),
        ],
        out_specs=pl.BlockSpec((tq, gw), q_map),
        out_shape=jax.ShapeDtypeStruct((n_rows, d_q), BF16),
        compiler_params=_params(("parallel", "parallel", "arbitrary")),
        name="attention",
    )(qk, qk, vcb, qk, vcb)


def _merge_kernel(attn_ref, cb_ref, z_ref, zprev_ref, znext_ref, cw_ref, wa_ref, wc_ref, ga_ref, gc_ref,
                  o_ref, conv_scr, *, rows, seq, ctx_len, chunk):
    i = pl.program_id(0)

    @pl.when(pl.program_id(1) == 0)
    def _():
        tm, d = conv_scr.shape
        seq_len = jnp.where(i < rows.lat_tiles, seq, ctx_len)
        r = lax.broadcasted_iota(I32, (tm, chunk), 0)
        pos = (r + i * tm) & (seq_len - 1)
        first, last = pos == 0, pos == seq_len - 1
        for c0 in range(0, d, chunk):
            cols = slice(c0, c0 + chunk)
            z = z_ref[:, cols].astype(F32)
            z_prev = zprev_ref[:, cols].astype(F32)
            z_next = znext_ref[:, cols].astype(F32)
            before = jnp.where(r == 0, z_prev[7:8], pltpu.roll(z, 1, 0))
            after = jnp.where(r == tm - 1, z_next[0:1], pltpu.roll(z, tm - 1, 0))
            before = jnp.where(first, 0.0, before)
            after = jnp.where(last, 0.0, after)
            conv = cw_ref[0:1, cols] * before + cw_ref[1:2, cols] * z + cw_ref[2:3, cols] * after
            conv_scr[:, cols] = (cb_ref[:, cols].astype(F32) * conv).astype(BF16)

    a = jnp.dot(attn_ref[...], wa_ref[...], preferred_element_type=F32)
    c = jnp.dot(conv_scr[...], wc_ref[...], preferred_element_type=F32)
    o_ref[...] = (ga_ref[...].astype(F32) * a + gc_ref[...].astype(F32) * c).astype(BF16)


def _merge(rows, attn, vcb, z, gates, conv_w, w_attn_o, w_conv_o, seq, ctx_len):
    d = attn.shape[1]
    tm = rows.tm
    tn = _tile(d, 512)
    n_j = d // tn
    halo = tm // 8
    last_halo = rows.n_rows // 8 - 1
    return pl.pallas_call(
        functools.partial(_merge_kernel, rows=rows, seq=seq, ctx_len=ctx_len, chunk=_tile(d, 512)),
        grid=(rows.n_tiles, n_j),
        in_specs=[
            pl.BlockSpec((tm, d), lambda i, j: (i, 0)),
            pl.BlockSpec((tm, d), lambda i, j: (i, 0)),
            pl.BlockSpec((tm, d), lambda i, j: (i, 0)),
            pl.BlockSpec((8, d), lambda i, j: (jnp.maximum(i * halo - 1, 0), 0)),
            pl.BlockSpec((8, d), lambda i, j: (jnp.minimum((i + 1) * halo, last_halo), 0)),
            pl.BlockSpec((3, d), lambda i, j: (0, 0)),
            pl.BlockSpec((d, tn), lambda i, j: (0, j)),
            pl.BlockSpec((d, tn), lambda i, j: (0, j)),
            pl.BlockSpec((tm, tn), lambda i, j: (i, j)),
            pl.BlockSpec((tm, tn), lambda i, j: (i, n_j + j)),
        ],
        out_specs=pl.BlockSpec((tm, tn), lambda i, j: (i, j)),
        out_shape=jax.ShapeDtypeStruct((rows.n_rows, d), BF16),
        scratch_shapes=[pltpu.VMEM((tm, d), BF16)],
        compiler_params=_params(("parallel", "arbitrary")),
        name="merge",
    )(attn, vcb, z, z, z, conv_w, w_attn_o, w_conv_o, gates, gates)


def _layer_norm(r, g, b):
    mu = jnp.mean(r, axis=-1, keepdims=True)
    c = r - mu
    var = jnp.mean(c * c, axis=-1, keepdims=True)
    return c * lax.rsqrt(var + NORM_EPS) * g + b


def _pack_bf16_pair(lo, hi):
    lo_bits = lax.bitcast_convert_type(lo.astype(BF16).astype(F32), I32)
    hi_bits = lax.bitcast_convert_type(hi.astype(BF16).astype(F32), I32)
    return lax.shift_right_logical(lo_bits, jnp.full_like(lo_bits, 16)) | (hi_bits & jnp.int32(-65536))


def _unpack_bf16_pair(w):
    lo = lax.bitcast_convert_type(lax.shift_left(w, jnp.full_like(w, 16)), F32)
    hi = lax.bitcast_convert_type(w & jnp.int32(-65536), F32)
    return lo.astype(BF16), hi.astype(BF16)


def _mix_router_kernel(y_ref, wmix_ref, x_ref, g1_ref, lng_ref, lnb_ref, sc2_ref, sh2_ref, rw_ref, rb_ref,
                       x1_ref, vp_ref, meta_ref, cnt_ref, run_scr, *, alpha, n_experts):
    @pl.when(pl.program_id(0) == 0)
    def _():
        run_scr[...] = jnp.zeros_like(run_scr)

    mix = jnp.dot(y_ref[...], wmix_ref[...], preferred_element_type=F32)
    x1 = _layer_norm(alpha * x_ref[...] + g1_ref[...] * mix, lng_ref[...], lnb_ref[...])
    x1_ref[...] = x1
    v = x1 * (1.0 + sc2_ref[...]) + sh2_ref[...]
    tm, d = v.shape
    vp_ref[...] = _pack_bf16_pair(v[:, : d // 2], v[:, d // 2:])

    logits = jnp.dot(v, rw_ref[...], preferred_element_type=F32, precision=lax.Precision.HIGHEST) + rb_ref[...]
    lane = lax.broadcasted_iota(I32, (tm, n_experts), 1).astype(F32)
    work = logits
    sel, val = [], []
    for _ in range(TOP_K):
        m = jnp.max(work, axis=-1, keepdims=True)
        s = jnp.min(jnp.where(work == m, lane, float(n_experts)), axis=-1, keepdims=True)
        sel.append(s)
        val.append(m)
        work = jnp.where(lane == s, -jnp.inf, work)
    ex = [jnp.exp(v_k - val[0]) for v_k in val]
    den = ex[0] + ex[1] + ex[2] + ex[3]

    onehot = jnp.zeros((tm, n_experts), F32)
    for s in sel:
        onehot = onehot + jnp.where(lane == s, 1.0, 0.0)
    rr = lax.broadcasted_iota(I32, (tm, tm), 0)
    cc = lax.broadcasted_iota(I32, (tm, tm), 1)
    tri = jnp.where(cc <= rr, 1.0, 0.0).astype(BF16)
    incl = jnp.dot(tri, onehot.astype(BF16), preferred_element_type=F32)
    before = run_scr[...] + incl - onehot

    mlane = lax.broadcasted_iota(I32, (tm, LANES), 1)
    meta = jnp.zeros((tm, LANES), F32)
    for k in range(TOP_K):
        pos_k = jnp.sum(jnp.where(lane == sel[k], before, 0.0), axis=-1, keepdims=True)
        meta = jnp.where(mlane == META_IDX + k, sel[k], meta)
        meta = jnp.where(mlane == META_GATE + k, ex[k] / den, meta)
        meta = jnp.where(mlane == META_POS + k, pos_k, meta)
    meta_ref[...] = meta
    run_scr[...] = run_scr[...] + incl[tm - 1:tm, :]
    cnt_ref[...] = run_scr[...]


def _mix_router(rows, y, w_mix, x, g1, ln_g, ln_b, sc2, sh2, router_w, router_b, alpha):
    d = x.shape[1]
    tm = rows.tm
    e = router_w.shape[1]
    mod_spec = pl.BlockSpec((None, 1, d), lambda i: (rows.mod_row(i), 0, 0))
    vec_spec = pl.BlockSpec((1, d), lambda i: (0, 0))
    return pl.pallas_call(
        functools.partial(_mix_router_kernel, alpha=alpha, n_experts=e),
        grid=(rows.n_tiles,),
        in_specs=[
            pl.BlockSpec((tm, d), lambda i: (i, 0)),
            pl.BlockSpec((d, d), lambda i: (0, 0)),
            pl.BlockSpec((tm, d), lambda i: (i, 0)),
            mod_spec, vec_spec, vec_spec, mod_spec, mod_spec,
            pl.BlockSpec((d, e), lambda i: (0, 0)),
            pl.BlockSpec((1, e), lambda i: (0, 0)),
        ],
        out_specs=[
            pl.BlockSpec((tm, d), lambda i: (i, 0)),
            pl.BlockSpec((tm, d // 2), lambda i: (i, 0)),
            pl.BlockSpec((tm, LANES), lambda i: (i, 0)),
            pl.BlockSpec((1, e), lambda i: (0, 0)),
        ],
        out_shape=[
            jax.ShapeDtypeStruct((rows.n_rows, d), F32),
            jax.ShapeDtypeStruct((rows.n_rows, d // 2), I32),
            jax.ShapeDtypeStruct((rows.n_rows, LANES), F32),
            jax.ShapeDtypeStruct((1, e), F32),
        ],
        scratch_shapes=[pltpu.VMEM((1, e), F32)],
        compiler_params=_params(("arbitrary",)),
        name="mix_ln_router",
    )(y, w_mix, x, g1, ln_g.reshape(1, d), ln_b.reshape(1, d), sc2, sh2, router_w, router_b.reshape(1, e))


def _dispatch_kernel(dest_ref, vp_ref, xg_in_ref, xg_ref, sem):
    del xg_in_ref
    tm = vp_ref.shape[0]

    def row_copy(r, d):
        return pltpu.make_async_copy(vp_ref.at[pl.ds(r, 1), :], xg_ref.at[pl.ds(d, 1), :], sem)

    def issue(r, carry):
        for k in range(TOP_K):
            row_copy(r, dest_ref[r * TOP_K + k]).start()
        return carry

    lax.fori_loop(0, tm, issue, 0)

    def drain(r, carry):
        row_copy(0, 0).wait()
        return carry

    lax.fori_loop(0, tm * TOP_K, drain, 0)


def _dispatch(vp, dest_flat, n_slots, tm):
    n_rows, half = vp.shape
    xg0 = jnp.zeros((n_slots, half), I32)
    return pl.pallas_call(
        _dispatch_kernel,
        grid=(n_rows // tm,),
        in_specs=[
            pl.BlockSpec((tm * TOP_K,), lambda i: (i,), memory_space=pltpu.SMEM),
            pl.BlockSpec((tm, half), lambda i: (i, 0)),
            pl.BlockSpec(memory_space=pl.ANY),
        ],
        out_specs=pl.BlockSpec(memory_space=pl.ANY),
        out_shape=jax.ShapeDtypeStruct((n_slots, half), I32),
        scratch_shapes=[pltpu.SemaphoreType.DMA(())],
        input_output_aliases={2: 0},
        compiler_params=_params(("arbitrary",)),
        name="dispatch",
    )(dest_flat, vp, xg0)


def _expert_kernel(be_ref, nused_ref, xg_ref, wup_ref, bup_ref, wdn_ref, bdn_ref, y_ref, x_scr):
    del be_ref
    i = pl.program_id(0)

    @pl.when(i < nused_ref[0])
    def _():
        half = xg_ref.shape[1]
        lo, hi = _unpack_bf16_pair(xg_ref[...])
        x_scr[:, :half] = lo
        x_scr[:, half:] = hi
        h = jnp.dot(x_scr[...], wup_ref[...], preferred_element_type=F32) + bup_ref[...]
        f = h.shape[1] // 2
        glu = jnp.minimum(h[:, :f], SWIGLU_LIMIT)
        lin = jnp.clip(h[:, f:], -SWIGLU_LIMIT, SWIGLU_LIMIT)
        act = glu * jax.nn.sigmoid(SWIGLU_ALPHA * glu) * (lin + 1.0)
        y_ref[...] = jnp.dot(act.astype(BF16), wdn_ref[...], preferred_element_type=F32) + bdn_ref[...]

    @pl.when(i >= nused_ref[0])
    def _():
        y_ref[...] = jnp.zeros_like(y_ref)


def _experts(xg, block_e, n_used, w_up, b_up, w_down, b_down):
    n_slots, half = xg.shape
    e, d, f2 = w_up.shape
    bm = EXPERT_ROWS
    n_blocks = n_slots // bm

    def x_map(i, be, nu):
        return (jnp.minimum(i, nu[0] - 1), 0)

    grid_spec = pltpu.PrefetchScalarGridSpec(
        num_scalar_prefetch=2,
        grid=(n_blocks,),
        in_specs=[
            pl.BlockSpec((bm, half), x_map),
            pl.BlockSpec((None, d, f2), lambda i, be, nu: (be[i], 0, 0)),
            pl.BlockSpec((None, 1, f2), lambda i, be, nu: (be[i], 0, 0)),
            pl.BlockSpec((None, f2 // 2, d), lambda i, be, nu: (be[i], 0, 0)),
            pl.BlockSpec((None, 1, d), lambda i, be, nu: (be[i], 0, 0)),
        ],
        out_specs=pl.BlockSpec((bm, d), lambda i, be, nu: (i, 0)),
        scratch_shapes=[pltpu.VMEM((bm, d), BF16)],
    )
    return pl.pallas_call(
        _expert_kernel,
        grid_spec=grid_spec,
        out_shape=jax.ShapeDtypeStruct((n_slots, d), F32),
        compiler_params=_params(("arbitrary",)),
        name="experts",
    )(block_e, n_used, xg, w_up, b_up.reshape(e, 1, f2), w_down, b_down.reshape(e, 1, d))


def _combine_kernel(dcur_ref, dnext_ref, meta_ref, x1_ref, g2_ref, lng_ref, lnb_ref, y_hbm,
                    o_ref, buf, sem, *, alpha):
    i = pl.program_id(0)
    n = pl.num_programs(0)
    tm = x1_ref.shape[0]

    def row_copy(dref, slot, r, k):
        d = dref[r * TOP_K + k]
        return pltpu.make_async_copy(y_hbm.at[pl.ds(d, 1), :], buf.at[slot, k, pl.ds(r, 1), :], sem.at[slot])

    def issue_tile(dref, slot):
        def body(r, carry):
            for k in range(TOP_K):
                row_copy(dref, slot, r, k).start()
            return carry
        lax.fori_loop(0, tm, body, 0)

    @pl.when(i == 0)
    def _():
        issue_tile(dcur_ref, 0)

    @pl.when(i + 1 < n)
    def _():
        issue_tile(dnext_ref, (i + 1) % 2)

    slot = i % 2

    def drain(r, carry):
        pltpu.make_async_copy(y_hbm.at[pl.ds(0, 1), :], buf.at[slot, 0, pl.ds(0, 1), :], sem.at[slot]).wait()
        return carry

    lax.fori_loop(0, tm * TOP_K, drain, 0)

    meta = meta_ref[...]
    f = jnp.zeros(x1_ref.shape, F32)
    for k in range(TOP_K):
        f = f + meta[:, META_GATE + k:META_GATE + k + 1] * buf[slot, k]
    o_ref[...] = _layer_norm(alpha * x1_ref[...] + g2_ref[...] * f, lng_ref[...], lnb_ref[...])


def _combine(rows, dest_flat, meta, x1, g2, ln_g, ln_b, y_slots, alpha):
    d = x1.shape[1]
    tm = rows.tm
    last = rows.n_tiles - 1
    vec_spec = pl.BlockSpec((1, d), lambda i: (0, 0))
    return pl.pallas_call(
        functools.partial(_combine_kernel, alpha=alpha),
        grid=(rows.n_tiles,),
        in_specs=[
            pl.BlockSpec((tm * TOP_K,), lambda i: (i,), memory_space=pltpu.SMEM),
            pl.BlockSpec((tm * TOP_K,), lambda i: (jnp.minimum(i + 1, last),), memory_space=pltpu.SMEM),
            pl.BlockSpec((tm, LANES), lambda i: (i, 0)),
            pl.BlockSpec((tm, d), lambda i: (i, 0)),
            pl.BlockSpec((None, 1, d), lambda i: (rows.mod_row(i), 0, 0)),
            vec_spec, vec_spec,
            pl.BlockSpec(memory_space=pl.ANY),
        ],
        out_specs=pl.BlockSpec((tm, d), lambda i: (i, 0)),
        out_shape=jax.ShapeDtypeStruct((rows.n_rows, d), F32),
        scratch_shapes=[pltpu.VMEM((2, TOP_K, tm, d), F32), pltpu.SemaphoreType.DMA((2,))],
        compiler_params=_params(("arbitrary",)),
        name="combine_ln",
    )(dest_flat, dest_flat, meta, x1, g2, ln_g.reshape(1, d), ln_b.reshape(1, d), y_slots)


def _slot_layout(meta, counts, n_rows):
    e = counts.shape[1]
    bm = EXPERT_ROWS
    cnt = counts[0].astype(I32)
    padded = (cnt + bm - 1) // bm * bm
    pend = jnp.cumsum(padded)
    pstart = pend - padded
    idx = meta[:, META_IDX:META_IDX + TOP_K].astype(I32)
    pos = meta[:, META_POS:META_POS + TOP_K].astype(I32)
    onehot = idx[..., None] == jnp.arange(e, dtype=I32)
    dest = jnp.sum(jnp.where(onehot, pstart, 0), axis=-1) + pos
    n_blocks = -(-(n_rows * TOP_K + e * (bm - 1)) // bm)
    n_used = pend[-1] // bm
    blk = jnp.minimum(jnp.arange(n_blocks, dtype=I32), n_used - 1) * bm
    block_e = jnp.minimum(jnp.sum(pend[None, :] <= blk[:, None], axis=-1), e - 1).astype(I32)
    return dest.reshape(-1), block_e, n_used.reshape(1).astype(I32), n_blocks * bm


def _rope_tables(seq, pad_rows):
    rows = seq // GRID_W
    row = jnp.repeat(jnp.arange(rows, dtype=I32), GRID_W).astype(F32)
    col = jnp.tile(jnp.arange(GRID_W, dtype=I32), rows).astype(F32)
    half = HEAD_DIM // 2
    inv = ROPE_THETA ** (-jnp.arange(0, half, 2, dtype=F32) / half)
    ang = jnp.concatenate([row[:, None] * inv, col[:, None] * inv], axis=-1)
    cos = jnp.repeat(jnp.cos(ang), 2, axis=-1)
    sin = jnp.repeat(jnp.sin(ang), 2, axis=-1) * jnp.tile(jnp.array([-1.0, 1.0], F32), half)
    cos = jnp.concatenate([cos, jnp.ones((pad_rows, HEAD_DIM), F32)], axis=0)
    sin = jnp.concatenate([sin, jnp.zeros((pad_rows, HEAD_DIM), F32)], axis=0)
    return cos, sin


def kernel(x, c, ctx, c_ctx, ada_w, ada_b, w_in, q_norm_g, k_norm_g, conv_w, w_attn_o, w_conv_o, w_mix_o,
           ln1_g, ln1_b, router_w, router_b, w_up, b_up, w_down, b_down, ln2_g, ln2_b):
    batch, seq, d = x.shape
    ctx_len = ctx.shape[1]
    depth = ada_w.shape[0]
    n_q = d // HEAD_DIM
    d_q, d_kv, d_conv = d, n_q // GQA_GROUP * HEAD_DIM, d
    o_k, o_v = d_q, d_q + d_kv
    o_cb = o_v + d_kv
    o_cc, o_cx = o_cb + d_conv, o_cb + 2 * d_conv
    o_ga = o_cx + d_conv
    n_lat, n_ctx = batch * seq, batch * ctx_len
    n_tok = n_lat + n_ctx
    alpha = (2 * depth) ** 0.25
    assert seq & (seq - 1) == 0 and ctx_len & (ctx_len - 1) == 0 and seq % ctx_len == 0 and seq % GRID_W == 0

    tn = next(t for t in (512, 256, 128) if all(o % t == 0 for o in (o_k, o_v, o_cb, o_cc, o_cx, o_ga)))
    row_unit = math.gcd(seq, n_ctx)
    tm_proj, tm_merge, tm_small = _tile(row_unit, 1024), _tile(row_unit, 512), _tile(row_unit, 256)

    mod_rows = -(-(batch + 1) // 8) * 8
    cvec = jnp.zeros((mod_rows, d), F32).at[:batch].set(c).at[batch].set(c_ctx)
    mod = _adaln(cvec, ada_w, ada_b).reshape(depth, mod_rows, 6, 1, d)

    cos, sin = _rope_tables(seq, tm_proj)
    scale = HEAD_DIM ** -0.5
    state = jnp.concatenate([x.reshape(n_lat, d), ctx.reshape(n_ctx, d)], axis=0)

    for l in range(depth):
        last = l == depth - 1
        sh1, sc1, g1, sh2, sc2, g2 = [mod[l, :, m] for m in range(6)]
        w_l = w_in[l].astype(BF16)
        gains = jnp.concatenate([jnp.tile(q_norm_g[l] * scale, n_q), jnp.tile(k_norm_g[l], d_kv // HEAD_DIM)])[None]
        n_main = n_lat if last else n_tok

        rows_all = _Rows(n_tok, tm_proj, batch, seq, n_lat)
        rows_main = _Rows(n_main, tm_proj, batch, seq, n_lat)
        qk = _proj("qk", rows_all, state, sc1, sh1, w_l, [0], d_q + d_kv, tn, extra=(gains, cos, sin), name="proj_qk")
        n_vt, n_vcb = d_kv // tn, (d_kv + d_conv) // tn
        vcb = _proj("plain", rows_all, state, sc1, sh1, w_l, [o_v], d_kv + d_conv, tn,
                    out_map=lambda i, j: (i, (j + n_vcb - n_vt) % n_vcb), name="proj_vcb")
        z = _proj("product", rows_main, state, sc1, sh1, w_l, [o_cc, o_cx], d_conv, tn, name="proj_z")
        gates = _proj("sigmoid", rows_main, state, sc1, sh1, w_l, [o_ga], 2 * d, tn, name="proj_gates")

        attn = _attention(qk, vcb, batch, seq, ctx_len, d_q, d_conv, ctx_queries=not last)

        rows_m = _Rows(n_main, tm_merge, batch, seq, n_lat)
        y = _merge(rows_m, attn, vcb, z, gates, conv_w[l], w_attn_o[l].astype(BF16), w_conv_o[l].astype(BF16),
                   seq, ctx_len)

        rows_s = _Rows(n_main, tm_small, batch, seq, n_lat)
        x1, vp, meta, counts = _mix_router(rows_s, y, w_mix_o[l].astype(BF16), state, g1, ln1_g[l], ln1_b[l],
                                           sc2, sh2, router_w[l], router_b[l], alpha)
        dest, block_e, n_used, n_slots = _slot_layout(meta, counts, n_main)
        xg = _dispatch(vp, dest, n_slots, tm_small)
        y_slots = _experts(xg, block_e, n_used, w_up[l].astype(BF16), b_up[l], w_down[l].astype(BF16), b_down[l])
        state = _combine(rows_s, dest, meta, x1, g2, ln2_g[l], ln2_b[l], y_slots, alpha)

    return state[:n_lat].reshape(batch, seq, d)
```

```python
import functools
import math

import jax
import jax.numpy as jnp
from jax import lax
from jax.experimental import pallas as pl
from jax.experimental.pallas import tpu as pltpu

F32 = jnp.float32
BF16 = jnp.bfloat16
I32 = jnp.int32

LANES = 128
HEAD_DIM = 128
GQA_GROUP = 4
GRID_W = 64
TOP_K = 4
SWIGLU_ALPHA = 1.702
SWIGLU_LIMIT = 7.0
ROPE_THETA = 10000.0
NORM_EPS = 1e-6
VMEM_LIMIT = 56 * 1024 * 1024
EXPERT_ROWS = 256
META_IDX, META_GATE, META_POS = 0, 4, 8
DMA_UNROLL = 8


def _params(sem, vmem=VMEM_LIMIT):
    return pltpu.CompilerParams(dimension_semantics=sem, vmem_limit_bytes=vmem)


def _tile(n, pref):
    if n <= pref:
        return n
    t = pref - pref % 8
    while n % t:
        t -= 8
    return t


def _adaln_kernel(c_ref, w_ref, b_ref, o_ref):
    c = c_ref[...]
    s = (c * jax.nn.sigmoid(c)).astype(BF16)
    o_ref[...] = jnp.dot(s, w_ref[...].astype(BF16), preferred_element_type=F32) + b_ref[...]


def _adaln(cvec, ada_w, ada_b):
    depth, d, n = ada_w.shape
    r = cvec.shape[0]
    tn = _tile(n, 1024)
    return pl.pallas_call(
        _adaln_kernel,
        grid=(depth, n // tn),
        in_specs=[
            pl.BlockSpec((r, d), lambda l, j: (0, 0)),
            pl.BlockSpec((None, d, tn), lambda l, j: (l, 0, j)),
            pl.BlockSpec((None, 1, tn), lambda l, j: (l, 0, j)),
        ],
        out_specs=pl.BlockSpec((None, r, tn), lambda l, j: (l, 0, j)),
        out_shape=jax.ShapeDtypeStruct((depth, r, n), F32),
        compiler_params=_params(("parallel", "parallel")),
        name="adaln",
    )(cvec, ada_w, ada_b.reshape(depth, 1, n))


def _modulate(u_scr, x_ref, sc_ref, sh_ref):
    @pl.when(pl.program_id(1) == 0)
    def _():
        u_scr[...] = (x_ref[...] * (1.0 + sc_ref[...]) + sh_ref[...]).astype(BF16)


def _proj_qk_kernel(x_ref, sc_ref, sh_ref, w_ref, ta_ref, tb_ref, o_ref, u_scr):
    _modulate(u_scr, x_ref, sc_ref, sh_ref)
    acc = jnp.dot(u_scr[...], w_ref[...], preferred_element_type=F32)
    tn = acc.shape[1]
    ta = ta_ref[...]
    tb = tb_ref[...]
    rr = lax.broadcasted_iota(I32, (HEAD_DIM, HEAD_DIM), 0)
    cc = lax.broadcasted_iota(I32, (HEAD_DIM, HEAD_DIM), 1)
    swap = jnp.where((rr ^ 1) == cc, 1.0, 0.0).astype(BF16)
    mean_w = jnp.full((HEAD_DIM, HEAD_DIM), 1.0 / HEAD_DIM, BF16)
    for h in range(tn // HEAD_DIM):
        cols = slice(h * HEAD_DIM, (h + 1) * HEAD_DIM)
        a = acc[:, cols]
        ms = jnp.dot((a * a).astype(BF16), mean_w, preferred_element_type=F32)
        partner = jnp.dot(a.astype(BF16), swap, preferred_element_type=F32)
        o_ref[:, cols] = (lax.rsqrt(ms + NORM_EPS) * (a * ta + partner * tb)).astype(BF16)


def _proj_plain_kernel(x_ref, sc_ref, sh_ref, w_ref, o_ref, u_scr):
    _modulate(u_scr, x_ref, sc_ref, sh_ref)
    o_ref[...] = jnp.dot(u_scr[...], w_ref[...], preferred_element_type=F32).astype(BF16)


def _proj_sigmoid_kernel(x_ref, sc_ref, sh_ref, w_ref, o_ref, u_scr):
    _modulate(u_scr, x_ref, sc_ref, sh_ref)
    acc = jnp.dot(u_scr[...], w_ref[...], preferred_element_type=F32)
    o_ref[...] = jax.nn.sigmoid(acc).astype(BF16)


def _proj_product_kernel(x_ref, sc_ref, sh_ref, wc_ref, wx_ref, o_ref, u_scr):
    _modulate(u_scr, x_ref, sc_ref, sh_ref)
    u = u_scr[...]
    c = jnp.dot(u, wc_ref[...], preferred_element_type=F32)
    xin = jnp.dot(u, wx_ref[...], preferred_element_type=F32)
    o_ref[...] = (c * xin).astype(BF16)


class _Rows:
    def __init__(self, n_rows, tm, batch, seq, n_lat):
        assert n_rows % tm == 0 and seq % tm == 0 and n_lat % tm == 0
        self.n_rows, self.tm, self.batch = n_rows, tm, batch
        self.n_tiles = n_rows // tm
        self.lat_tiles = n_lat // tm
        self.per_seq = seq // tm

    def mod_row(self, i):
        return jnp.where(i < self.lat_tiles, i // self.per_seq, self.batch)

    def rope_blk(self, i):
        return jnp.where(i < self.lat_tiles, i % self.per_seq, self.per_seq)


def _proj(kind, rows, x, sc, sh, w, layer, col_offs, n_cols, tn, extra=(), out_map=None, name="proj"):
    d = x.shape[1]
    tm = rows.tm
    n_j = n_cols // tn
    kernels = {"qk": _proj_qk_kernel, "plain": _proj_plain_kernel,
               "sigmoid": _proj_sigmoid_kernel, "product": _proj_product_kernel}
    in_specs = [
        pl.BlockSpec((tm, d), lambda i, j: (i, 0)),
        pl.BlockSpec((None, 1, d), lambda i, j: (rows.mod_row(i), 0, 0)),
        pl.BlockSpec((None, 1, d), lambda i, j: (rows.mod_row(i), 0, 0)),
    ]
    operands = [x, sc, sh]
    for off in col_offs:
        assert off % tn == 0
        in_specs.append(pl.BlockSpec((None, d, tn), functools.partial(lambda i, j, o: (layer, 0, o + j), o=off // tn)))
        operands.append(w)
    if kind == "qk":
        tab_a, tab_b, q_tiles = extra
        tab_spec = pl.BlockSpec((None, tm, HEAD_DIM), lambda i, j: (jnp.where(j < q_tiles, 0, 1), rows.rope_blk(i), 0))
        in_specs += [tab_spec, tab_spec]
        operands += [tab_a, tab_b]
    if out_map is None:
        out_map = lambda i, j: (i, j)
    return pl.pallas_call(
        kernels[kind],
        grid=(rows.n_tiles, n_j),
        in_specs=in_specs,
        out_specs=pl.BlockSpec((tm, tn), out_map),
        out_shape=jax.ShapeDtypeStruct((rows.n_rows, n_cols), BF16),
        scratch_shapes=[pltpu.VMEM((tm, d), BF16)],
        compiler_params=_params(("parallel", "arbitrary")),
        name=name,
    )(*operands)


def _attn_heads(q_ref, kl_ref, kc_ref, vl_ext, vc_ext, o_ref, with_latent):
    nt = (((1,), (1,)), ((), ()))
    for h in range(GQA_GROUP):
        cols = slice(h * HEAD_DIM, (h + 1) * HEAD_DIM)
        q = q_ref[:, cols]
        s_c = lax.dot_general(q, kc_ref[...], nt, preferred_element_type=F32)
        m = jnp.max(s_c, axis=-1, keepdims=True)
        if with_latent:
            s_l = lax.dot_general(q, kl_ref[...], nt, preferred_element_type=F32)
            m = jnp.maximum(m, jnp.max(s_l, axis=-1, keepdims=True))
        p_c = jnp.exp((s_c - m).astype(BF16))
        o = jnp.dot(p_c, vc_ext[...], preferred_element_type=F32)
        if with_latent:
            p_l = jnp.exp((s_l - m).astype(BF16))
            o = o + jnp.dot(p_l, vl_ext[...], preferred_element_type=F32)
        o_ref[:, cols] = (o[:, :HEAD_DIM] / o[:, HEAD_DIM:]).astype(BF16)


def _attn_kernel(q_ref, kl_ref, vl_ref, kc_ref, vc_ref, o_ref, vl_ext, vc_ext, *, lat_steps, ctx_step):
    qi = pl.program_id(2)

    @pl.when(qi == 0)
    def _():
        vl_ext[:, :HEAD_DIM] = vl_ref[...]
        vl_ext[:, HEAD_DIM:] = jnp.ones_like(vl_ref)
        vc_ext[:, :HEAD_DIM] = vc_ref[...]
        vc_ext[:, HEAD_DIM:] = jnp.ones_like(vc_ref)

    if not ctx_step:
        _attn_heads(q_ref, kl_ref, kc_ref, vl_ext, vc_ext, o_ref, True)
        return

    @pl.when(qi < lat_steps)
    def _():
        _attn_heads(q_ref, kl_ref, kc_ref, vl_ext, vc_ext, o_ref, True)

    @pl.when(qi >= lat_steps)
    def _():
        _attn_heads(q_ref, kl_ref, kc_ref, vl_ext, vc_ext, o_ref, False)


def _attention(qk, vcb, batch, seq, ctx_len, d_q, d_conv, ctx_queries):
    n_kv = d_q // HEAD_DIM // GQA_GROUP
    tq = ctx_len
    lat_steps = seq // tq
    n_lat_blk = batch * seq // tq
    gw = GQA_GROUP * HEAD_DIM
    k_col = d_q // HEAD_DIM
    v_col = d_conv // HEAD_DIM
    n_rows = batch * seq + (batch * ctx_len if ctx_queries else 0)

    def q_map(b, g, qi):
        return (jnp.where(qi < lat_steps, b * lat_steps + qi, n_lat_blk + b), g)

    return pl.pallas_call(
        functools.partial(_attn_kernel, lat_steps=lat_steps, ctx_step=ctx_queries),
        grid=(batch, n_kv, lat_steps + (1 if ctx_queries else 0)),
        in_specs=[
            pl.BlockSpec((tq, gw), q_map),
            pl.BlockSpec((seq, HEAD_DIM), lambda b, g, qi: (b, k_col + g)),
            pl.BlockSpec((seq, HEAD_DIM), lambda b, g, qi: (b, v_col + g)),
            pl.BlockSpec((ctx_len, HEAD_DIM), lambda b, g, qi: (n_lat_blk + b, k_col + g)),
            pl.BlockSpec((ctx_len, HEAD_DIM), lambda b, g, qi: (n_lat_blk + b, v_col + g)),
        ],
        out_specs=pl.BlockSpec((tq, gw), q_map),
        out_shape=jax.ShapeDtypeStruct((n_rows, d_q), BF16),
        scratch_shapes=[pltpu.VMEM((seq, 2 * HEAD_DIM), BF16), pltpu.VMEM((ctx_len, 2 * HEAD_DIM), BF16)],
        compiler_params=_params(("parallel", "parallel", "arbitrary")),
        name="attention",
    )(qk, qk, vcb, qk, vcb)


def _merge_kernel(attn_ref, cb_ref, z_ref, zprev_ref, znext_ref, cw_ref, wa_ref, wc_ref, ga_ref, gc_ref,
                  o_ref, conv_scr, *, rows, seq, ctx_len, chunk):
    i = pl.program_id(0)

    @pl.when(pl.program_id(1) == 0)
    def _():
        tm, d = conv_scr.shape
        seq_len = jnp.where(i < rows.lat_tiles, seq, ctx_len)
        r = lax.broadcasted_iota(I32, (tm, chunk), 0)
        pos = (r + i * tm) & (seq_len - 1)
        first, last = pos == 0, pos == seq_len - 1
        for c0 in range(0, d, chunk):
            cols = slice(c0, c0 + chunk)
            z = z_ref[:, cols].astype(F32)
            z_prev = zprev_ref[:, cols].astype(F32)
            z_next = znext_ref[:, cols].astype(F32)
            before = jnp.where(r == 0, z_prev[7:8], pltpu.roll(z, 1, 0))
            after = jnp.where(r == tm - 1, z_next[0:1], pltpu.roll(z, tm - 1, 0))
            before = jnp.where(first, 0.0, before)
            after = jnp.where(last, 0.0, after)
            conv = cw_ref[0:1, cols] * before + cw_ref[1:2, cols] * z + cw_ref[2:3, cols] * after
            conv_scr[:, cols] = (cb_ref[:, cols].astype(F32) * conv).astype(BF16)

    a = jnp.dot(attn_ref[...], wa_ref[...], preferred_element_type=F32)
    c = jnp.dot(conv_scr[...], wc_ref[...], preferred_element_type=F32)
    o_ref[...] = (ga_ref[...].astype(F32) * a + gc_ref[...].astype(F32) * c).astype(BF16)


def _merge(rows, attn, vcb, z, gates, conv_w, w_attn_o, w_conv_o, layer, seq, ctx_len):
    d = attn.shape[1]
    tm = rows.tm
    tn = _tile(d, 512)
    n_j = d // tn
    halo = tm // 8
    last_halo = rows.n_rows // 8 - 1
    return pl.pallas_call(
        functools.partial(_merge_kernel, rows=rows, seq=seq, ctx_len=ctx_len, chunk=_tile(d, 512)),
        grid=(rows.n_tiles, n_j),
        in_specs=[
            pl.BlockSpec((tm, d), lambda i, j: (i, 0)),
            pl.BlockSpec((tm, d), lambda i, j: (i, 0)),
            pl.BlockSpec((tm, d), lambda i, j: (i, 0)),
            pl.BlockSpec((8, d), lambda i, j: (jnp.maximum(i * halo - 1, 0), 0)),
            pl.BlockSpec((8, d), lambda i, j: (jnp.minimum((i + 1) * halo, last_halo), 0)),
            pl.BlockSpec((None, 3, d), lambda i, j: (layer, 0, 0)),
            pl.BlockSpec((None, d, tn), lambda i, j: (layer, 0, j)),
            pl.BlockSpec((None, d, tn), lambda i, j: (layer, 0, j)),
            pl.BlockSpec((tm, tn), lambda i, j: (i, j)),
            pl.BlockSpec((tm, tn), lambda i, j: (i, n_j + j)),
        ],
        out_specs=pl.BlockSpec((tm, tn), lambda i, j: (i, j)),
        out_shape=jax.ShapeDtypeStruct((rows.n_rows, d), BF16),
        scratch_shapes=[pltpu.VMEM((tm, d), BF16)],
        compiler_params=_params(("parallel", "arbitrary")),
        name="merge",
    )(attn, vcb, z, z, z, conv_w, w_attn_o, w_conv_o, gates, gates)


def _layer_norm(r, g, b):
    mu = jnp.mean(r, axis=-1, keepdims=True)
    c = r - mu
    var = jnp.mean(c * c, axis=-1, keepdims=True)
    return c * lax.rsqrt(var + NORM_EPS) * g + b


def _pack_bf16_pair(lo, hi):
    lo_bits = lax.bitcast_convert_type(lo.astype(BF16).astype(F32), I32)
    hi_bits = lax.bitcast_convert_type(hi.astype(BF16).astype(F32), I32)
    return lax.shift_right_logical(lo_bits, jnp.full_like(lo_bits, 16)) | (hi_bits & jnp.int32(-65536))


def _unpack_bf16_pair(w):
    lo = lax.bitcast_convert_type(lax.shift_left(w, jnp.full_like(w, 16)), F32)
    hi = lax.bitcast_convert_type(w & jnp.int32(-65536), F32)
    return lo, hi


def _store_token_rows(ref, packed):
    tm, half = packed.shape
    n_chunks = half // LANES
    for c in range(n_chunks):
        ref[pl.ds(c, tm, stride=n_chunks), :] = packed[:, c * LANES:(c + 1) * LANES]


def _load_token_chunk(ref, lead, c, tm, n_chunks):
    return ref[(*lead, pl.ds(c, tm, stride=n_chunks), slice(None))]


def _mix_router_kernel(y_ref, wmix_ref, x_ref, g1_ref, lng_ref, lnb_ref, sc2_ref, sh2_ref, rw2_ref, rwh_ref, rb_ref,
                       x1_ref, vp_ref, meta_ref, cnt_ref, run_scr, *, alpha, n_experts):
    @pl.when(pl.program_id(0) == 0)
    def _():
        run_scr[...] = jnp.zeros_like(run_scr)

    mix = jnp.dot(y_ref[...], wmix_ref[...], preferred_element_type=F32)
    x1 = _layer_norm(alpha * x_ref[...] + g1_ref[...] * mix, lng_ref[...], lnb_ref[...])
    x1_ref[...] = x1
    v = x1 * (1.0 + sc2_ref[...]) + sh2_ref[...]
    tm, d = v.shape
    _store_token_rows(vp_ref, _pack_bf16_pair(v[:, : d // 2], v[:, d // 2:]))

    v_hi = v.astype(BF16)
    v_lo = (v - v_hi.astype(F32)).astype(BF16)
    hh_hl = jnp.dot(v_hi, rw2_ref[...], preferred_element_type=F32)
    lh = jnp.dot(v_lo, rwh_ref[...], preferred_element_type=F32)
    logits = hh_hl[:, :n_experts] + hh_hl[:, n_experts:] + lh + rb_ref[...]
    lane = lax.broadcasted_iota(I32, (tm, n_experts), 1).astype(F32)
    work = logits
    sel, val = [], []
    for _ in range(TOP_K):
        m = jnp.max(work, axis=-1, keepdims=True)
        s = jnp.min(jnp.where(work == m, lane, float(n_experts)), axis=-1, keepdims=True)
        sel.append(s)
        val.append(m)
        work = jnp.where(lane == s, -jnp.inf, work)
    ex = [jnp.exp(v_k - val[0]) for v_k in val]
    den = ex[0] + ex[1] + ex[2] + ex[3]

    onehot = jnp.zeros((tm, n_experts), F32)
    for s in sel:
        onehot = onehot + jnp.where(lane == s, 1.0, 0.0)
    rr = lax.broadcasted_iota(I32, (tm, tm), 0)
    cc = lax.broadcasted_iota(I32, (tm, tm), 1)
    tri = jnp.where(cc <= rr, 1.0, 0.0).astype(BF16)
    incl = jnp.dot(tri, onehot.astype(BF16), preferred_element_type=F32)
    before = run_scr[...] + incl - onehot

    mlane = lax.broadcasted_iota(I32, (tm, LANES), 1)
    meta = jnp.zeros((tm, LANES), F32)
    for k in range(TOP_K):
        pos_k = jnp.sum(jnp.where(lane == sel[k], before, 0.0), axis=-1, keepdims=True)
        meta = jnp.where(mlane == META_IDX + k, sel[k], meta)
        meta = jnp.where(mlane == META_GATE + k, ex[k] / den, meta)
        meta = jnp.where(mlane == META_POS + k, pos_k, meta)
    meta_ref[...] = meta
    run_scr[...] = run_scr[...] + incl[tm - 1:tm, :]
    cnt_ref[...] = run_scr[...]


def _mix_router(rows, y, w_mix, layer, x, g1, ln_g, ln_b, sc2, sh2, router_w, router_b, alpha):
    d = x.shape[1]
    tm = rows.tm
    e = router_w.shape[1]
    n_chunks = d // 2 // LANES
    rw_hi = router_w.astype(BF16)
    rw_lo = (router_w - rw_hi.astype(F32)).astype(BF16)
    mod_spec = pl.BlockSpec((None, 1, d), lambda i: (rows.mod_row(i), 0, 0))
    vec_spec = pl.BlockSpec((1, d), lambda i: (0, 0))
    return pl.pallas_call(
        functools.partial(_mix_router_kernel, alpha=alpha, n_experts=e),
        grid=(rows.n_tiles,),
        in_specs=[
            pl.BlockSpec((tm, d), lambda i: (i, 0)),
            pl.BlockSpec((None, d, d), lambda i: (layer, 0, 0)),
            pl.BlockSpec((tm, d), lambda i: (i, 0)),
            mod_spec, vec_spec, vec_spec, mod_spec, mod_spec,
            pl.BlockSpec((d, 2 * e), lambda i: (0, 0)),
            pl.BlockSpec((d, e), lambda i: (0, 0)),
            pl.BlockSpec((1, e), lambda i: (0, 0)),
        ],
        out_specs=[
            pl.BlockSpec((tm, d), lambda i: (i, 0)),
            pl.BlockSpec((tm * n_chunks, LANES), lambda i: (i, 0)),
            pl.BlockSpec((tm, LANES), lambda i: (i, 0)),
            pl.BlockSpec((1, e), lambda i: (0, 0)),
        ],
        out_shape=[
            jax.ShapeDtypeStruct((rows.n_rows, d), F32),
            jax.ShapeDtypeStruct((rows.n_rows * n_chunks, LANES), I32),
            jax.ShapeDtypeStruct((rows.n_rows, LANES), F32),
            jax.ShapeDtypeStruct((1, e), F32),
        ],
        scratch_shapes=[pltpu.VMEM((1, e), F32)],
        compiler_params=_params(("arbitrary",)),
        name="mix_ln_router",
    )(y, w_mix, x, g1, ln_g.reshape(1, d), ln_b.reshape(1, d), sc2, sh2,
      jnp.concatenate([rw_hi, rw_lo], axis=1), rw_hi, router_b.reshape(1, e))


def _dispatch_kernel(dest_ref, vp_ref, xg_in_ref, xg_ref, sem, *, tm, n_chunks):
    del xg_in_ref

    def row_copy(r, d):
        src = vp_ref.at[pl.ds(pl.multiple_of(r * n_chunks, n_chunks), n_chunks), :]
        dst = xg_ref.at[pl.ds(pl.multiple_of(d * n_chunks, n_chunks), n_chunks), :]
        return pltpu.make_async_copy(src, dst, sem)

    def issue(r, carry):
        for k in range(TOP_K):
            row_copy(r, dest_ref[r * TOP_K + k]).start()
        return carry

    lax.fori_loop(0, tm, issue, 0, unroll=DMA_UNROLL)

    def drain(r, carry):
        row_copy(0, 0).wait()
        return carry

    lax.fori_loop(0, tm * TOP_K, drain, 0, unroll=DMA_UNROLL)


def _dispatch(vp, dest_flat, n_slots, tm, n_chunks):
    n_rows = vp.shape[0] // n_chunks
    xg0 = jnp.zeros((n_slots * n_chunks, LANES), I32)
    return pl.pallas_call(
        functools.partial(_dispatch_kernel, tm=tm, n_chunks=n_chunks),
        grid=(n_rows // tm,),
        in_specs=[
            pl.BlockSpec((tm * TOP_K,), lambda i: (i,), memory_space=pltpu.SMEM),
            pl.BlockSpec((tm * n_chunks, LANES), lambda i: (i, 0)),
            pl.BlockSpec(memory_space=pl.ANY),
        ],
        out_specs=pl.BlockSpec(memory_space=pl.ANY),
        out_shape=jax.ShapeDtypeStruct((n_slots * n_chunks, LANES), I32),
        scratch_shapes=[pltpu.SemaphoreType.DMA(())],
        input_output_aliases={2: 0},
        compiler_params=_params(("arbitrary",)),
        name="dispatch",
    )(dest_flat, vp, xg0)


def _expert_kernel(be_ref, nused_ref, xg_ref, wup_ref, bup_ref, wdn_ref, bdn_ref, y_ref, x_scr):
    del be_ref
    i = pl.program_id(0)

    @pl.when(i < nused_ref[0])
    def _():
        bm, d = x_scr.shape
        half = d // 2
        n_chunks = half // LANES
        for c in range(n_chunks):
            lo, hi = _unpack_bf16_pair(_load_token_chunk(xg_ref, (), c, bm, n_chunks))
            x_scr[:, c * LANES:(c + 1) * LANES] = lo.astype(BF16)
            x_scr[:, half + c * LANES:half + (c + 1) * LANES] = hi.astype(BF16)
        h = jnp.dot(x_scr[...], wup_ref[...], preferred_element_type=F32) + bup_ref[...]
        f = h.shape[1] // 2
        glu = jnp.minimum(h[:, :f], SWIGLU_LIMIT)
        lin = jnp.clip(h[:, f:], -SWIGLU_LIMIT, SWIGLU_LIMIT)
        act = glu * jax.nn.sigmoid(SWIGLU_ALPHA * glu) * (lin + 1.0)
        y = jnp.dot(act.astype(BF16), wdn_ref[...], preferred_element_type=F32) + bdn_ref[...]
        _store_token_rows(y_ref, _pack_bf16_pair(y[:, :half], y[:, half:]))

    @pl.when(i >= nused_ref[0])
    def _():
        y_ref[...] = jnp.zeros_like(y_ref)


def _experts(xg, block_e, n_used, w_up, b_up, w_down, b_down, layer):
    _, e, d, f2 = w_up.shape
    n_chunks = d // 2 // LANES
    bm = EXPERT_ROWS
    n_blocks = xg.shape[0] // n_chunks // bm

    def x_map(i, be, nu):
        return (jnp.minimum(i, nu[0] - 1), 0)

    def w_map(i, be, nu):
        return (layer, be[i], 0, 0)

    grid_spec = pltpu.PrefetchScalarGridSpec(
        num_scalar_prefetch=2,
        grid=(n_blocks,),
        in_specs=[
            pl.BlockSpec((bm * n_chunks, LANES), x_map),
            pl.BlockSpec((None, None, d, f2), w_map),
            pl.BlockSpec((None, None, 1, f2), w_map),
            pl.BlockSpec((None, None, f2 // 2, d), w_map),
            pl.BlockSpec((None, None, 1, d), w_map),
        ],
        out_specs=pl.BlockSpec((bm * n_chunks, LANES), lambda i, be, nu: (i, 0)),
        scratch_shapes=[pltpu.VMEM((bm, d), BF16)],
    )
    depth = w_up.shape[0]
    return pl.pallas_call(
        _expert_kernel,
        grid_spec=grid_spec,
        out_shape=jax.ShapeDtypeStruct(xg.shape, I32),
        compiler_params=_params(("arbitrary",)),
        name="experts",
    )(block_e, n_used, xg, w_up, b_up.reshape(depth, e, 1, f2), w_down, b_down.reshape(depth, e, 1, d))


def _combine_kernel(dcur_ref, dnext_ref, meta_ref, x1_ref, g2_ref, lng_ref, lnb_ref, y_hbm,
                    o_ref, buf, f_scr, sem, *, alpha):
    i = pl.program_id(0)
    n = pl.num_programs(0)
    tm, d = x1_ref.shape
    half = d // 2
    n_chunks = half // LANES

    def row_copy(d_slot, slot, r, k):
        src = y_hbm.at[pl.ds(pl.multiple_of(d_slot * n_chunks, n_chunks), n_chunks), :]
        dst = buf.at[slot, k, pl.ds(pl.multiple_of(r * n_chunks, n_chunks), n_chunks), :]
        return pltpu.make_async_copy(src, dst, sem.at[slot])

    def issue_tile(dref, slot):
        def body(r, carry):
            for k in range(TOP_K):
                row_copy(dref[r * TOP_K + k], slot, r, k).start()
            return carry
        lax.fori_loop(0, tm, body, 0, unroll=DMA_UNROLL)

    @pl.when(i == 0)
    def _():
        issue_tile(dcur_ref, 0)

    @pl.when(i + 1 < n)
    def _():
        issue_tile(dnext_ref, (i + 1) % 2)

    slot = i % 2

    def drain(r, carry):
        row_copy(0, slot, 0, 0).wait()
        return carry

    lax.fori_loop(0, tm * TOP_K, drain, 0, unroll=DMA_UNROLL)

    meta = meta_ref[...]
    gate = [meta[:, META_GATE + k:META_GATE + k + 1] for k in range(TOP_K)]
    for c in range(n_chunks):
        f_lo = jnp.zeros((tm, LANES), F32)
        f_hi = jnp.zeros((tm, LANES), F32)
        for k in range(TOP_K):
            lo, hi = _unpack_bf16_pair(_load_token_chunk(buf, (slot, k), c, tm, n_chunks))
            f_lo = f_lo + gate[k] * lo
            f_hi = f_hi + gate[k] * hi
        f_scr[:, c * LANES:(c + 1) * LANES] = f_lo
        f_scr[:, half + c * LANES:half + (c + 1) * LANES] = f_hi
    o_ref[...] = _layer_norm(alpha * x1_ref[...] + g2_ref[...] * f_scr[...], lng_ref[...], lnb_ref[...])


def _combine(rows, dest_flat, meta, x1, g2, ln_g, ln_b, y_slots, alpha):
    d = x1.shape[1]
    tm = rows.tm
    n_chunks = d // 2 // LANES
    last = rows.n_tiles - 1
    vec_spec = pl.BlockSpec((1, d), lambda i: (0, 0))
    return pl.pallas_call(
        functools.partial(_combine_kernel, alpha=alpha),
        grid=(rows.n_tiles,),
        in_specs=[
            pl.BlockSpec((tm * TOP_K,), lambda i: (i,), memory_space=pltpu.SMEM),
            pl.BlockSpec((tm * TOP_K,), lambda i: (jnp.minimum(i + 1, last),), memory_space=pltpu.SMEM),
            pl.BlockSpec((tm, LANES), lambda i: (i, 0)),
            pl.BlockSpec((tm, d), lambda i: (i, 0)),
            pl.BlockSpec((None, 1, d), lambda i: (rows.mod_row(i), 0, 0)),
            vec_spec, vec_spec,
            pl.BlockSpec(memory_space=pl.ANY),
        ],
        out_specs=pl.BlockSpec((tm, d), lambda i: (i, 0)),
        out_shape=jax.ShapeDtypeStruct((rows.n_rows, d), F32),
        scratch_shapes=[pltpu.VMEM((2, TOP_K, tm * n_chunks, LANES), I32), pltpu.VMEM((tm, d), F32),
                        pltpu.SemaphoreType.DMA((2,))],
        compiler_params=_params(("arbitrary",)),
        name="combine_ln",
    )(dest_flat, dest_flat, meta, x1, g2, ln_g.reshape(1, d), ln_b.reshape(1, d), y_slots)


def _slot_layout(meta, counts, n_rows):
    e = counts.shape[1]
    bm = EXPERT_ROWS
    cnt = counts[0].astype(I32)
    padded = (cnt + bm - 1) // bm * bm
    pend = jnp.cumsum(padded)
    pstart = pend - padded
    idx = meta[:, META_IDX:META_IDX + TOP_K].astype(I32)
    pos = meta[:, META_POS:META_POS + TOP_K].astype(I32)
    onehot = idx[..., None] == jnp.arange(e, dtype=I32)
    dest = jnp.sum(jnp.where(onehot, pstart, 0), axis=-1) + pos
    n_blocks = -(-(n_rows * TOP_K + e * (bm - 1)) // bm)
    n_used = pend[-1] // bm
    blk = jnp.minimum(jnp.arange(n_blocks, dtype=I32), n_used - 1) * bm
    block_e = jnp.minimum(jnp.sum(pend[None, :] <= blk[:, None], axis=-1), e - 1).astype(I32)
    return dest.reshape(-1), block_e, n_used.reshape(1).astype(I32), n_blocks * bm


def _rope_tables(seq, pad_rows):
    rows = seq // GRID_W
    row = jnp.repeat(jnp.arange(rows, dtype=I32), GRID_W).astype(F32)
    col = jnp.tile(jnp.arange(GRID_W, dtype=I32), rows).astype(F32)
    half = HEAD_DIM // 2
    inv = ROPE_THETA ** (-jnp.arange(0, half, 2, dtype=F32) / half)
    ang = jnp.concatenate([row[:, None] * inv, col[:, None] * inv], axis=-1)
    cos = jnp.repeat(jnp.cos(ang), 2, axis=-1)
    sin = jnp.repeat(jnp.sin(ang), 2, axis=-1) * jnp.tile(jnp.array([-1.0, 1.0], F32), half)
    cos = jnp.concatenate([cos, jnp.ones((pad_rows, HEAD_DIM), F32)], axis=0)
    sin = jnp.concatenate([sin, jnp.zeros((pad_rows, HEAD_DIM), F32)], axis=0)
    return cos, sin


def kernel(x, c, ctx, c_ctx, ada_w, ada_b, w_in, q_norm_g, k_norm_g, conv_w, w_attn_o, w_conv_o, w_mix_o,
           ln1_g, ln1_b, router_w, router_b, w_up, b_up, w_down, b_down, ln2_g, ln2_b):
    batch, seq, d = x.shape
    ctx_len = ctx.shape[1]
    depth = ada_w.shape[0]
    n_q = d // HEAD_DIM
    d_q, d_kv, d_conv = d, n_q // GQA_GROUP * HEAD_DIM, d
    o_k, o_v = d_q, d_q + d_kv
    o_cb = o_v + d_kv
    o_cc, o_cx = o_cb + d_conv, o_cb + 2 * d_conv
    o_ga = o_cx + d_conv
    n_lat, n_ctx = batch * seq, batch * ctx_len
    n_tok = n_lat + n_ctx
    alpha = (2 * depth) ** 0.25
    assert seq & (seq - 1) == 0 and ctx_len & (ctx_len - 1) == 0 and seq % ctx_len == 0 and seq % GRID_W == 0

    tn = next(t for t in (512, 256, 128) if all(o % t == 0 for o in (o_k, o_v, o_cb, o_cc, o_cx, o_ga)))
    row_unit = math.gcd(seq, n_ctx)
    tm_proj, tm_merge, tm_small = _tile(row_unit, 1024), _tile(row_unit, 512), _tile(row_unit, 256)

    mod_rows = -(-(batch + 1) // 8) * 8
    cvec = jnp.zeros((mod_rows, d), F32).at[:batch].set(c).at[batch].set(c_ctx)
    mod = _adaln(cvec, ada_w, ada_b).reshape(depth, mod_rows, 6, 1, d)

    cos, sin = _rope_tables(seq, tm_proj)
    scale = HEAD_DIM ** -0.5
    state = jnp.concatenate([x.reshape(n_lat, d), ctx.reshape(n_ctx, d)], axis=0)
    n_chunks = d // 2 // LANES
    w_in_b, w_attn_b, w_conv_b, w_mix_b = (w.astype(BF16) for w in (w_in, w_attn_o, w_conv_o, w_mix_o))
    w_up_b, w_down_b = w_up.astype(BF16), w_down.astype(BF16)
    pair = jnp.arange(HEAD_DIM) ^ 1

    for l in range(depth):
        last = l == depth - 1
        sh1, sc1, g1, sh2, sc2, g2 = [mod[l, :, m] for m in range(6)]
        gq, gk = q_norm_g[l] * scale, k_norm_g[l]
        tab_a = jnp.stack([cos * gq, cos * gk])
        tab_b = jnp.stack([sin * gq[pair], sin * gk[pair]])
        n_main = n_lat if last else n_tok

        rows_all = _Rows(n_tok, tm_proj, batch, seq, n_lat)
        rows_main = _Rows(n_main, tm_proj, batch, seq, n_lat)
        qk = _proj("qk", rows_all, state, sc1, sh1, w_in_b, l, [0], d_q + d_kv, tn,
                   extra=(tab_a, tab_b, d_q // tn), name="proj_qk")
        n_vt, n_vcb = d_kv // tn, (d_kv + d_conv) // tn
        vcb = _proj("plain", rows_all, state, sc1, sh1, w_in_b, l, [o_v], d_kv + d_conv, tn,
                    out_map=lambda i, j: (i, (j + n_vcb - n_vt) % n_vcb), name="proj_vcb")
        z = _proj("product", rows_main, state, sc1, sh1, w_in_b, l, [o_cc, o_cx], d_conv, tn, name="proj_z")
        gates = _proj("sigmoid", rows_main, state, sc1, sh1, w_in_b, l, [o_ga], 2 * d, tn, name="proj_gates")

        attn = _attention(qk, vcb, batch, seq, ctx_len, d_q, d_conv, ctx_queries=not last)

        rows_m = _Rows(n_main, tm_merge, batch, seq, n_lat)
        y = _merge(rows_m, attn, vcb, z, gates, conv_w, w_attn_b, w_conv_b, l, seq, ctx_len)

        rows_s = _Rows(n_main, tm_small, batch, seq, n_lat)
        x1, vp, meta, counts = _mix_router(rows_s, y, w_mix_b, l, state, g1, ln1_g[l], ln1_b[l],
                                           sc2, sh2, router_w[l], router_b[l], alpha)
        dest, block_e, n_used, n_slots = _slot_layout(meta, counts, n_main)
        xg = _dispatch(vp, dest, n_slots, tm_small, n_chunks)
        y_slots = _experts(xg, block_e, n_used, w_up_b, b_up, w_down_b, b_down, l)
        state = _combine(rows_s, dest, meta, x1, g2, ln2_g[l], ln2_b[l], y_slots, alpha)

    return state[:n_lat].reshape(batch, seq, d)
```

```python
import functools
import math

import jax
import jax.numpy as jnp
from jax import lax
from jax.experimental import pallas as pl
from jax.experimental.pallas import tpu as pltpu

F32 = jnp.float32
BF16 = jnp.bfloat16
I32 = jnp.int32

LANES = 128
HEAD_DIM = 128
GQA_GROUP = 4
GRID_W = 64
TOP_K = 4
SWIGLU_ALPHA = 1.702
SWIGLU_LIMIT = 7.0
ROPE_THETA = 10000.0
NORM_EPS = 1e-6
VMEM_LIMIT = 56 * 1024 * 1024
EXPERT_ROWS = 256
META_IDX, META_GATE, META_POS = 0, 4, 8
DMA_UNROLL = 8


def _params(sem, vmem=VMEM_LIMIT):
    return pltpu.CompilerParams(dimension_semantics=sem, vmem_limit_bytes=vmem)


def _tile(n, pref):
    if n <= pref:
        return n
    t = pref - pref % 8
    while n % t:
        t -= 8
    return t


def _adaln_kernel(c_ref, w_ref, b_ref, o_ref):
    c = c_ref[...]
    s = (c * jax.nn.sigmoid(c)).astype(BF16)
    o_ref[...] = jnp.dot(s, w_ref[...].astype(BF16), preferred_element_type=F32) + b_ref[...]


def _adaln(cvec, ada_w, ada_b):
    depth, d, n = ada_w.shape
    r = cvec.shape[0]
    tn = _tile(n, 1024)
    return pl.pallas_call(
        _adaln_kernel,
        grid=(depth, n // tn),
        in_specs=[
            pl.BlockSpec((r, d), lambda l, j: (0, 0)),
            pl.BlockSpec((None, d, tn), lambda l, j: (l, 0, j)),
            pl.BlockSpec((None, 1, tn), lambda l, j: (l, 0, j)),
        ],
        out_specs=pl.BlockSpec((None, r, tn), lambda l, j: (l, 0, j)),
        out_shape=jax.ShapeDtypeStruct((depth, r, n), F32),
        compiler_params=_params(("parallel", "parallel")),
        name="adaln",
    )(cvec, ada_w, ada_b.reshape(depth, 1, n))


def _modulate(u_scr, x_ref, sc_ref, sh_ref):
    @pl.when(pl.program_id(1) == 0)
    def _():
        u_scr[...] = (x_ref[...] * (1.0 + sc_ref[...]) + sh_ref[...]).astype(BF16)


def _proj_qk_kernel(x_ref, sc_ref, sh_ref, w_ref, ta_ref, tb_ref, o_ref, u_scr):
    _modulate(u_scr, x_ref, sc_ref, sh_ref)
    acc = jnp.dot(u_scr[...], w_ref[...], preferred_element_type=F32)
    tn = acc.shape[1]
    ta = ta_ref[...]
    tb = tb_ref[...]
    rr = lax.broadcasted_iota(I32, (HEAD_DIM, HEAD_DIM), 0)
    cc = lax.broadcasted_iota(I32, (HEAD_DIM, HEAD_DIM), 1)
    swap = jnp.where((rr ^ 1) == cc, 1.0, 0.0).astype(BF16)
    mean_w = jnp.full((HEAD_DIM, HEAD_DIM), 1.0 / HEAD_DIM, BF16)
    for h in range(tn // HEAD_DIM):
        cols = slice(h * HEAD_DIM, (h + 1) * HEAD_DIM)
        a = acc[:, cols]
        ms = jnp.dot((a * a).astype(BF16), mean_w, preferred_element_type=F32)
        partner = jnp.dot(a.astype(BF16), swap, preferred_element_type=F32)
        o_ref[:, cols] = (lax.rsqrt(ms + NORM_EPS) * (a * ta + partner * tb)).astype(BF16)


def _proj_plain_kernel(x_ref, sc_ref, sh_ref, w_ref, o_ref, u_scr):
    _modulate(u_scr, x_ref, sc_ref, sh_ref)
    o_ref[...] = jnp.dot(u_scr[...], w_ref[...], preferred_element_type=F32).astype(BF16)


def _sigmoid(x):
    return 0.5 * jnp.tanh(0.5 * x) + 0.5


def _proj_sigmoid_kernel(x_ref, sc_ref, sh_ref, w_ref, o_ref, u_scr):
    _modulate(u_scr, x_ref, sc_ref, sh_ref)
    acc = jnp.dot(u_scr[...], w_ref[...], preferred_element_type=F32)
    o_ref[...] = _sigmoid(acc).astype(BF16)


def _proj_product_kernel(x_ref, sc_ref, sh_ref, wc_ref, wx_ref, o_ref, u_scr):
    _modulate(u_scr, x_ref, sc_ref, sh_ref)
    u = u_scr[...]
    c = jnp.dot(u, wc_ref[...], preferred_element_type=F32)
    xin = jnp.dot(u, wx_ref[...], preferred_element_type=F32)
    o_ref[...] = (c * xin).astype(BF16)


class _Rows:
    def __init__(self, n_rows, tm, batch, seq, n_lat):
        assert n_rows % tm == 0 and seq % tm == 0 and n_lat % tm == 0
        self.n_rows, self.tm, self.batch = n_rows, tm, batch
        self.n_tiles = n_rows // tm
        self.lat_tiles = n_lat // tm
        self.per_seq = seq // tm

    def mod_row(self, i):
        return jnp.where(i < self.lat_tiles, i // self.per_seq, self.batch)

    def rope_blk(self, i):
        return jnp.where(i < self.lat_tiles, i % self.per_seq, self.per_seq)


def _proj(kind, rows, x, sc, sh, w, layer, col_offs, n_cols, tn, extra=(), out_map=None, name="proj"):
    d = x.shape[1]
    tm = rows.tm
    n_j = n_cols // tn
    kernels = {"qk": _proj_qk_kernel, "plain": _proj_plain_kernel,
               "sigmoid": _proj_sigmoid_kernel, "product": _proj_product_kernel}
    in_specs = [
        pl.BlockSpec((tm, d), lambda i, j: (i, 0)),
        pl.BlockSpec((None, 1, d), lambda i, j: (rows.mod_row(i), 0, 0)),
        pl.BlockSpec((None, 1, d), lambda i, j: (rows.mod_row(i), 0, 0)),
    ]
    operands = [x, sc, sh]
    for off in col_offs:
        assert off % tn == 0
        in_specs.append(pl.BlockSpec((None, d, tn), functools.partial(lambda i, j, o: (layer, 0, o + j), o=off // tn)))
        operands.append(w)
    if kind == "qk":
        tab_a, tab_b, q_tiles = extra
        tab_spec = pl.BlockSpec((None, tm, HEAD_DIM), lambda i, j: (jnp.where(j < q_tiles, 0, 1), rows.rope_blk(i), 0))
        in_specs += [tab_spec, tab_spec]
        operands += [tab_a, tab_b]
    if out_map is None:
        out_map = lambda i, j: (i, j)
    return pl.pallas_call(
        kernels[kind],
        grid=(rows.n_tiles, n_j),
        in_specs=in_specs,
        out_specs=pl.BlockSpec((tm, tn), out_map),
        out_shape=jax.ShapeDtypeStruct((rows.n_rows, n_cols), BF16),
        scratch_shapes=[pltpu.VMEM((tm, d), BF16)],
        compiler_params=_params(("parallel", "arbitrary")),
        name=name,
    )(*operands)


def _attn_heads(q_ref, kl_ref, kc_ref, vl_ext, vc_ext, o_ref, with_latent):
    nt = (((1,), (1,)), ((), ()))
    for h in range(GQA_GROUP):
        cols = slice(h * HEAD_DIM, (h + 1) * HEAD_DIM)
        q = q_ref[:, cols]
        s_c = lax.dot_general(q, kc_ref[...], nt, preferred_element_type=F32)
        m = jnp.max(s_c, axis=-1, keepdims=True)
        if with_latent:
            s_l = lax.dot_general(q, kl_ref[...], nt, preferred_element_type=F32)
            m = jnp.maximum(m, jnp.max(s_l, axis=-1, keepdims=True))
        p_c = jnp.exp((s_c - m).astype(BF16))
        o = jnp.dot(p_c, vc_ext[...], preferred_element_type=F32)
        if with_latent:
            p_l = jnp.exp((s_l - m).astype(BF16))
            o = o + jnp.dot(p_l, vl_ext[...], preferred_element_type=F32)
        o_ref[:, cols] = (o[:, :HEAD_DIM] / o[:, HEAD_DIM:]).astype(BF16)


def _attn_kernel(q_ref, kl_ref, vl_ref, kc_ref, vc_ref, o_ref, vl_ext, vc_ext, *, lat_steps, ctx_step):
    qi = pl.program_id(2)

    @pl.when(qi == 0)
    def _():
        vl_ext[:, :HEAD_DIM] = vl_ref[...]
        vl_ext[:, HEAD_DIM:] = jnp.ones_like(vl_ref)
        vc_ext[:, :HEAD_DIM] = vc_ref[...]
        vc_ext[:, HEAD_DIM:] = jnp.ones_like(vc_ref)

    if not ctx_step:
        _attn_heads(q_ref, kl_ref, kc_ref, vl_ext, vc_ext, o_ref, True)
        return

    @pl.when(qi < lat_steps)
    def _():
        _attn_heads(q_ref, kl_ref, kc_ref, vl_ext, vc_ext, o_ref, True)

    @pl.when(qi >= lat_steps)
    def _():
        _attn_heads(q_ref, kl_ref, kc_ref, vl_ext, vc_ext, o_ref, False)


def _attention(qk, vcb, batch, seq, ctx_len, d_q, d_conv, ctx_queries):
    n_kv = d_q // HEAD_DIM // GQA_GROUP
    tq = ctx_len
    lat_steps = seq // tq
    n_lat_blk = batch * seq // tq
    gw = GQA_GROUP * HEAD_DIM
    k_col = d_q // HEAD_DIM
    v_col = d_conv // HEAD_DIM
    n_rows = batch * seq + (batch * ctx_len if ctx_queries else 0)

    def q_map(b, g, qi):
        return (jnp.where(qi < lat_steps, b * lat_steps + qi, n_lat_blk + b), g)

    return pl.pallas_call(
        functools.partial(_attn_kernel, lat_steps=lat_steps, ctx_step=ctx_queries),
        grid=(batch, n_kv, lat_steps + (1 if ctx_queries else 0)),
        in_specs=[
            pl.BlockSpec((tq, gw), q_map),
            pl.BlockSpec((seq, HEAD_DIM), lambda b, g, qi: (b, k_col + g)),
            pl.BlockSpec((seq, HEAD_DIM), lambda b, g, qi: (b, v_col + g)),
            pl.BlockSpec((ctx_len, HEAD_DIM), lambda b, g, qi: (n_lat_blk + b, k_col + g)),
            pl.BlockSpec((ctx_len, HEAD_DIM), lambda b, g, qi: (n_lat_blk + b, v_col + g)),
        ],
        out_specs=pl.BlockSpec((tq, gw), q_map),
        out_shape=jax.ShapeDtypeStruct((n_rows, d_q), BF16),
        scratch_shapes=[pltpu.VMEM((seq, 2 * HEAD_DIM), BF16), pltpu.VMEM((ctx_len, 2 * HEAD_DIM), BF16)],
        compiler_params=_params(("parallel", "parallel", "arbitrary")),
        name="attention",
    )(qk, qk, vcb, qk, vcb)


def _merge_kernel(attn_ref, cb_ref, z_ref, zprev_ref, znext_ref, cw_ref, wa_ref, wc_ref, ga_ref, gc_ref,
                  o_ref, conv_scr, *, rows, seq, ctx_len, chunk):
    i = pl.program_id(0)

    @pl.when(pl.program_id(1) == 0)
    def _():
        tm, d = conv_scr.shape
        seq_len = jnp.where(i < rows.lat_tiles, seq, ctx_len)
        r = lax.broadcasted_iota(I32, (tm, chunk), 0)
        pos = (r + i * tm) & (seq_len - 1)
        first, last = pos == 0, pos == seq_len - 1
        for c0 in range(0, d, chunk):
            cols = slice(c0, c0 + chunk)
            z = z_ref[:, cols].astype(F32)
            z_prev = zprev_ref[:, cols].astype(F32)
            z_next = znext_ref[:, cols].astype(F32)
            before = jnp.where(r == 0, z_prev[7:8], pltpu.roll(z, 1, 0))
            after = jnp.where(r == tm - 1, z_next[0:1], pltpu.roll(z, tm - 1, 0))
            before = jnp.where(first, 0.0, before)
            after = jnp.where(last, 0.0, after)
            conv = cw_ref[0:1, cols] * before + cw_ref[1:2, cols] * z + cw_ref[2:3, cols] * after
            conv_scr[:, cols] = (cb_ref[:, cols].astype(F32) * conv).astype(BF16)

    a = jnp.dot(attn_ref[...], wa_ref[...], preferred_element_type=F32)
    c = jnp.dot(conv_scr[...], wc_ref[...], preferred_element_type=F32)
    o_ref[...] = (ga_ref[...].astype(F32) * a + gc_ref[...].astype(F32) * c).astype(BF16)


def _merge(rows, attn, vcb, z, gates, conv_w, w_attn_o, w_conv_o, layer, seq, ctx_len):
    d = attn.shape[1]
    tm = rows.tm
    tn = _tile(d, 512)
    n_j = d // tn
    halo = tm // 8
    last_halo = rows.n_rows // 8 - 1
    return pl.pallas_call(
        functools.partial(_merge_kernel, rows=rows, seq=seq, ctx_len=ctx_len, chunk=_tile(d, 512)),
        grid=(rows.n_tiles, n_j),
        in_specs=[
            pl.BlockSpec((tm, d), lambda i, j: (i, 0)),
            pl.BlockSpec((tm, d), lambda i, j: (i, 0)),
            pl.BlockSpec((tm, d), lambda i, j: (i, 0)),
            pl.BlockSpec((8, d), lambda i, j: (jnp.maximum(i * halo - 1, 0), 0)),
            pl.BlockSpec((8, d), lambda i, j: (jnp.minimum((i + 1) * halo, last_halo), 0)),
            pl.BlockSpec((None, 3, d), lambda i, j: (layer, 0, 0)),
            pl.BlockSpec((None, d, tn), lambda i, j: (layer, 0, j)),
            pl.BlockSpec((None, d, tn), lambda i, j: (layer, 0, j)),
            pl.BlockSpec((tm, tn), lambda i, j: (i, j)),
            pl.BlockSpec((tm, tn), lambda i, j: (i, n_j + j)),
        ],
        out_specs=pl.BlockSpec((tm, tn), lambda i, j: (i, j)),
        out_shape=jax.ShapeDtypeStruct((rows.n_rows, d), BF16),
        scratch_shapes=[pltpu.VMEM((tm, d), BF16)],
        compiler_params=_params(("parallel", "arbitrary")),
        name="merge",
    )(attn, vcb, z, z, z, conv_w, w_attn_o, w_conv_o, gates, gates)


def _layer_norm(r, g, b):
    mu = jnp.mean(r, axis=-1, keepdims=True)
    c = r - mu
    var = jnp.mean(c * c, axis=-1, keepdims=True)
    return c * lax.rsqrt(var + NORM_EPS) * g + b


def _pack_bf16_pair(lo, hi):
    lo_bits = lax.bitcast_convert_type(lo.astype(BF16).astype(F32), I32)
    hi_bits = lax.bitcast_convert_type(hi.astype(BF16).astype(F32), I32)
    return lax.shift_right_logical(lo_bits, jnp.full_like(lo_bits, 16)) | (hi_bits & jnp.int32(-65536))


def _unpack_bf16_pair(w):
    lo = lax.bitcast_convert_type(lax.shift_left(w, jnp.full_like(w, 16)), F32)
    hi = lax.bitcast_convert_type(w & jnp.int32(-65536), F32)
    return lo, hi


def _store_token_rows(ref, packed):
    tm, half = packed.shape
    n_chunks = half // LANES
    for c in range(n_chunks):
        ref[pl.ds(c, tm, stride=n_chunks), :] = packed[:, c * LANES:(c + 1) * LANES]


def _load_token_chunk(ref, lead, c, tm, n_chunks):
    return ref[(*lead, pl.ds(c, tm, stride=n_chunks), slice(None))]


def _mix_router_kernel(y_ref, wmix_ref, x_ref, g1_ref, lng_ref, lnb_ref, sc2_ref, sh2_ref, rw2_ref, rwh_ref, rb_ref,
                       x1_ref, vp_ref, meta_ref, cnt_ref, run_scr, *, alpha, n_experts):
    @pl.when(pl.program_id(0) == 0)
    def _():
        run_scr[...] = jnp.zeros_like(run_scr)

    mix = jnp.dot(y_ref[...], wmix_ref[...], preferred_element_type=F32)
    x1 = _layer_norm(alpha * x_ref[...] + g1_ref[...] * mix, lng_ref[...], lnb_ref[...])
    x1_ref[...] = x1
    v = x1 * (1.0 + sc2_ref[...]) + sh2_ref[...]
    tm, d = v.shape
    _store_token_rows(vp_ref, _pack_bf16_pair(v[:, : d // 2], v[:, d // 2:]))

    v_hi = v.astype(BF16)
    v_lo = (v - v_hi.astype(F32)).astype(BF16)
    hh_hl = jnp.dot(v_hi, rw2_ref[...], preferred_element_type=F32)
    lh = jnp.dot(v_lo, rwh_ref[...], preferred_element_type=F32)
    logits = hh_hl[:, :n_experts] + hh_hl[:, n_experts:] + lh + rb_ref[...]
    lane = lax.broadcasted_iota(I32, (tm, n_experts), 1).astype(F32)
    work = logits
    sel, val = [], []
    for _ in range(TOP_K):
        m = jnp.max(work, axis=-1, keepdims=True)
        s = jnp.min(jnp.where(work == m, lane, float(n_experts)), axis=-1, keepdims=True)
        sel.append(s)
        val.append(m)
        work = jnp.where(lane == s, -jnp.inf, work)
    ex = [jnp.exp(v_k - val[0]) for v_k in val]
    den = ex[0] + ex[1] + ex[2] + ex[3]

    onehot = jnp.zeros((tm, n_experts), F32)
    for s in sel:
        onehot = onehot + jnp.where(lane == s, 1.0, 0.0)
    rr = lax.broadcasted_iota(I32, (tm, tm), 0)
    cc = lax.broadcasted_iota(I32, (tm, tm), 1)
    tri = jnp.where(cc <= rr, 1.0, 0.0).astype(BF16)
    incl = jnp.dot(tri, onehot.astype(BF16), preferred_element_type=F32)
    before = run_scr[...] + incl - onehot

    mlane = lax.broadcasted_iota(I32, (tm, LANES), 1)
    meta = jnp.zeros((tm, LANES), F32)
    for k in range(TOP_K):
        pos_k = jnp.sum(jnp.where(lane == sel[k], before, 0.0), axis=-1, keepdims=True)
        meta = jnp.where(mlane == META_IDX + k, sel[k], meta)
        meta = jnp.where(mlane == META_GATE + k, ex[k] / den, meta)
        meta = jnp.where(mlane == META_POS + k, pos_k, meta)
    meta_ref[...] = meta
    run_scr[...] = run_scr[...] + incl[tm - 1:tm, :]
    cnt_ref[...] = run_scr[...]


def _mix_router(rows, y, w_mix, layer, x, g1, ln_g, ln_b, sc2, sh2, router_w, router_b, alpha):
    d = x.shape[1]
    tm = rows.tm
    e = router_w.shape[1]
    n_chunks = d // 2 // LANES
    rw_hi = router_w.astype(BF16)
    rw_lo = (router_w - rw_hi.astype(F32)).astype(BF16)
    mod_spec = pl.BlockSpec((None, 1, d), lambda i: (rows.mod_row(i), 0, 0))
    vec_spec = pl.BlockSpec((1, d), lambda i: (0, 0))
    return pl.pallas_call(
        functools.partial(_mix_router_kernel, alpha=alpha, n_experts=e),
        grid=(rows.n_tiles,),
        in_specs=[
            pl.BlockSpec((tm, d), lambda i: (i, 0)),
            pl.BlockSpec((None, d, d), lambda i: (layer, 0, 0)),
            pl.BlockSpec((tm, d), lambda i: (i, 0)),
            mod_spec, vec_spec, vec_spec, mod_spec, mod_spec,
            pl.BlockSpec((d, 2 * e), lambda i: (0, 0)),
            pl.BlockSpec((d, e), lambda i: (0, 0)),
            pl.BlockSpec((1, e), lambda i: (0, 0)),
        ],
        out_specs=[
            pl.BlockSpec((tm, d), lambda i: (i, 0)),
            pl.BlockSpec((tm * n_chunks, LANES), lambda i: (i, 0)),
            pl.BlockSpec((tm, LANES), lambda i: (i, 0)),
            pl.BlockSpec((1, e), lambda i: (0, 0)),
        ],
        out_shape=[
            jax.ShapeDtypeStruct((rows.n_rows, d), F32),
            jax.ShapeDtypeStruct((rows.n_rows * n_chunks, LANES), I32),
            jax.ShapeDtypeStruct((rows.n_rows, LANES), F32),
            jax.ShapeDtypeStruct((1, e), F32),
        ],
        scratch_shapes=[pltpu.VMEM((1, e), F32)],
        compiler_params=_params(("arbitrary",)),
        name="mix_ln_router",
    )(y, w_mix, x, g1, ln_g.reshape(1, d), ln_b.reshape(1, d), sc2, sh2,
      jnp.concatenate([rw_hi, rw_lo], axis=1), rw_hi, router_b.reshape(1, e))


def _dispatch_kernel(pad_lo_ref, pad_hi_ref, dest_ref, vp_ref, xg_ref, zero_scr, sem, zero_sem, *, tm, n_chunks):
    def slot_rows(ref, s):
        return ref.at[pl.ds(pl.multiple_of(s * n_chunks, n_chunks), n_chunks), :]

    @pl.when(pl.program_id(0) == 0)
    def _():
        zero_scr[...] = jnp.zeros_like(zero_scr)

        def zero_copy(s):
            return pltpu.make_async_copy(zero_scr, slot_rows(xg_ref, s), zero_sem)

        def start(s, carry):
            zero_copy(s).start()
            return carry

        def wait(s, carry):
            zero_copy(s).wait()
            return carry

        for e in range(pad_lo_ref.shape[0]):
            lax.fori_loop(pad_lo_ref[e], pad_hi_ref[e], start, 0)
        for e in range(pad_lo_ref.shape[0]):
            lax.fori_loop(pad_lo_ref[e], pad_hi_ref[e], wait, 0)

    def row_copy(r, d):
        return pltpu.make_async_copy(slot_rows(vp_ref, r), slot_rows(xg_ref, d), sem)

    def issue(r, carry):
        for k in range(TOP_K):
            row_copy(r, dest_ref[r * TOP_K + k]).start(priority=k % 2)
        return carry

    lax.fori_loop(0, tm, issue, 0, unroll=DMA_UNROLL)

    def drain(r, carry):
        row_copy(0, 0).wait()
        return carry

    lax.fori_loop(0, tm * TOP_K, drain, 0, unroll=DMA_UNROLL)


def _dispatch(vp, dest_flat, pad_lo, pad_hi, n_slots, tm, n_chunks):
    n_rows = vp.shape[0] // n_chunks
    grid_spec = pltpu.PrefetchScalarGridSpec(
        num_scalar_prefetch=2,
        grid=(n_rows // tm,),
        in_specs=[
            pl.BlockSpec((tm * TOP_K,), lambda i, lo, hi: (i,), memory_space=pltpu.SMEM),
            pl.BlockSpec((tm * n_chunks, LANES), lambda i, lo, hi: (i, 0)),
        ],
        out_specs=pl.BlockSpec(memory_space=pl.ANY),
        scratch_shapes=[pltpu.VMEM((n_chunks, LANES), I32), pltpu.SemaphoreType.DMA(()), pltpu.SemaphoreType.DMA(())],
    )
    return pl.pallas_call(
        functools.partial(_dispatch_kernel, tm=tm, n_chunks=n_chunks),
        grid_spec=grid_spec,
        out_shape=jax.ShapeDtypeStruct((n_slots * n_chunks, LANES), I32),
        compiler_params=_params(("arbitrary",)),
        name="dispatch",
    )(pad_lo, pad_hi, dest_flat, vp)


def _expert_kernel(be_ref, nused_ref, xg_ref, wup_ref, bup_ref, wdn_ref, bdn_ref, y_ref, x_scr):
    del be_ref
    i = pl.program_id(0)

    @pl.when(i < nused_ref[0])
    def _():
        bm, d = x_scr.shape
        half = d // 2
        n_chunks = half // LANES
        for c in range(n_chunks):
            lo, hi = _unpack_bf16_pair(_load_token_chunk(xg_ref, (), c, bm, n_chunks))
            x_scr[:, c * LANES:(c + 1) * LANES] = lo.astype(BF16)
            x_scr[:, half + c * LANES:half + (c + 1) * LANES] = hi.astype(BF16)
        h = jnp.dot(x_scr[...], wup_ref[...], preferred_element_type=F32) + bup_ref[...]
        f = h.shape[1] // 2
        glu = jnp.minimum(h[:, :f], SWIGLU_LIMIT)
        lin = jnp.clip(h[:, f:], -SWIGLU_LIMIT, SWIGLU_LIMIT)
        act = glu * _sigmoid(SWIGLU_ALPHA * glu) * (lin + 1.0)
        y = jnp.dot(act.astype(BF16), wdn_ref[...], preferred_element_type=F32) + bdn_ref[...]
        _store_token_rows(y_ref, _pack_bf16_pair(y[:, :half], y[:, half:]))

    @pl.when(i >= nused_ref[0])
    def _():
        y_ref[...] = jnp.zeros_like(y_ref)


def _experts(xg, block_e, n_used, w_up, b_up, w_down, b_down, layer):
    _, e, d, f2 = w_up.shape
    n_chunks = d // 2 // LANES
    bm = EXPERT_ROWS
    n_blocks = xg.shape[0] // n_chunks // bm

    def x_map(i, be, nu):
        return (jnp.maximum(jnp.minimum(i, nu[0] - 1), 0), 0)

    def w_map(i, be, nu):
        return (layer, be[i], 0, 0)

    grid_spec = pltpu.PrefetchScalarGridSpec(
        num_scalar_prefetch=2,
        grid=(n_blocks,),
        in_specs=[
            pl.BlockSpec((bm * n_chunks, LANES), x_map),
            pl.BlockSpec((None, None, d, f2), w_map),
            pl.BlockSpec((None, None, 1, f2), w_map),
            pl.BlockSpec((None, None, f2 // 2, d), w_map),
            pl.BlockSpec((None, None, 1, d), w_map),
        ],
        out_specs=pl.BlockSpec((bm * n_chunks, LANES), lambda i, be, nu: (i, 0)),
        scratch_shapes=[pltpu.VMEM((bm, d), BF16)],
    )
    depth = w_up.shape[0]
    return pl.pallas_call(
        _expert_kernel,
        grid_spec=grid_spec,
        out_shape=jax.ShapeDtypeStruct(xg.shape, I32),
        compiler_params=_params(("arbitrary",)),
        name="experts",
    )(block_e, n_used, xg, w_up, b_up.reshape(depth, e, 1, f2), w_down, b_down.reshape(depth, e, 1, d))


def _combine_kernel(dcur_ref, dnext_ref, meta_ref, x1_ref, g2_ref, lng_ref, lnb_ref, y_hbm,
                    o_ref, buf, f_scr, sem, *, alpha):
    i = pl.program_id(0)
    n = pl.num_programs(0)
    tm, d = x1_ref.shape
    half = d // 2
    n_chunks = half // LANES

    def row_copy(d_slot, slot, r, k):
        src = y_hbm.at[pl.ds(pl.multiple_of(d_slot * n_chunks, n_chunks), n_chunks), :]
        dst = buf.at[slot, k, pl.ds(pl.multiple_of(r * n_chunks, n_chunks), n_chunks), :]
        return pltpu.make_async_copy(src, dst, sem.at[slot])

    def issue_tile(dref, slot):
        def body(r, carry):
            for k in range(TOP_K):
                row_copy(dref[r * TOP_K + k], slot, r, k).start(priority=k % 2)
            return carry
        lax.fori_loop(0, tm, body, 0, unroll=DMA_UNROLL)

    @pl.when(i == 0)
    def _():
        issue_tile(dcur_ref, 0)

    @pl.when(i + 1 < n)
    def _():
        issue_tile(dnext_ref, (i + 1) % 2)

    slot = i % 2

    def drain(r, carry):
        row_copy(0, slot, 0, 0).wait()
        return carry

    lax.fori_loop(0, tm * TOP_K, drain, 0, unroll=DMA_UNROLL)

    meta = meta_ref[...]
    gate = [meta[:, META_GATE + k:META_GATE + k + 1] for k in range(TOP_K)]
    for c in range(n_chunks):
        f_lo = jnp.zeros((tm, LANES), F32)
        f_hi = jnp.zeros((tm, LANES), F32)
        for k in range(TOP_K):
            lo, hi = _unpack_bf16_pair(_load_token_chunk(buf, (slot, k), c, tm, n_chunks))
            f_lo = f_lo + gate[k] * lo
            f_hi = f_hi + gate[k] * hi
        f_scr[:, c * LANES:(c + 1) * LANES] = f_lo
        f_scr[:, half + c * LANES:half + (c + 1) * LANES] = f_hi
    o_ref[...] = _layer_norm(alpha * x1_ref[...] + g2_ref[...] * f_scr[...], lng_ref[...], lnb_ref[...])


def _combine(rows, dest_flat, meta, x1, g2, ln_g, ln_b, y_slots, alpha):
    d = x1.shape[1]
    tm = rows.tm
    n_chunks = d // 2 // LANES
    last = rows.n_tiles - 1
    vec_spec = pl.BlockSpec((1, d), lambda i: (0, 0))
    return pl.pallas_call(
        functools.partial(_combine_kernel, alpha=alpha),
        grid=(rows.n_tiles,),
        in_specs=[
            pl.BlockSpec((tm * TOP_K,), lambda i: (i,), memory_space=pltpu.SMEM),
            pl.BlockSpec((tm * TOP_K,), lambda i: (jnp.minimum(i + 1, last),), memory_space=pltpu.SMEM),
            pl.BlockSpec((tm, LANES), lambda i: (i, 0)),
            pl.BlockSpec((tm, d), lambda i: (i, 0)),
            pl.BlockSpec((None, 1, d), lambda i: (rows.mod_row(i), 0, 0)),
            vec_spec, vec_spec,
            pl.BlockSpec(memory_space=pl.ANY),
        ],
        out_specs=pl.BlockSpec((tm, d), lambda i: (i, 0)),
        out_shape=jax.ShapeDtypeStruct((rows.n_rows, d), F32),
        scratch_shapes=[pltpu.VMEM((2, TOP_K, tm * n_chunks, LANES), I32), pltpu.VMEM((tm, d), F32),
                        pltpu.SemaphoreType.DMA((2,))],
        compiler_params=_params(("arbitrary",)),
        name="combine_ln",
    )(dest_flat, dest_flat, meta, x1, g2, ln_g.reshape(1, d), ln_b.reshape(1, d), y_slots)


def _slot_layout(meta, counts, n_rows):
    e = counts.shape[1]
    bm = EXPERT_ROWS
    cnt = counts[0].astype(I32)
    padded = (cnt + bm - 1) // bm * bm
    pend = jnp.cumsum(padded)
    pstart = pend - padded
    idx = meta[:, META_IDX:META_IDX + TOP_K].astype(I32)
    pos = meta[:, META_POS:META_POS + TOP_K].astype(I32)
    onehot = idx[..., None] == jnp.arange(e, dtype=I32)
    dest = jnp.sum(jnp.where(onehot, pstart, 0), axis=-1) + pos
    n_blocks = -(-(n_rows * TOP_K + e * (bm - 1)) // bm)
    n_used = pend[-1] // bm
    blk = jnp.minimum(jnp.arange(n_blocks, dtype=I32), n_used - 1) * bm
    block_e = jnp.minimum(jnp.sum(pend[None, :] <= blk[:, None], axis=-1), e - 1).astype(I32)
    pad_lo = jnp.concatenate([pstart + cnt, pend[-1:]]).astype(I32)
    pad_hi = jnp.concatenate([pend, jnp.full((1,), n_blocks * bm, I32)]).astype(I32)
    return dest.reshape(-1), block_e, n_used.reshape(1).astype(I32), pad_lo, pad_hi, n_blocks * bm


def _rope_tables(seq, pad_rows):
    rows = seq // GRID_W
    row = jnp.repeat(jnp.arange(rows, dtype=I32), GRID_W).astype(F32)
    col = jnp.tile(jnp.arange(GRID_W, dtype=I32), rows).astype(F32)
    half = HEAD_DIM // 2
    inv = ROPE_THETA ** (-jnp.arange(0, half, 2, dtype=F32) / half)
    ang = jnp.concatenate([row[:, None] * inv, col[:, None] * inv], axis=-1)
    cos = jnp.repeat(jnp.cos(ang), 2, axis=-1)
    sin = jnp.repeat(jnp.sin(ang), 2, axis=-1) * jnp.tile(jnp.array([-1.0, 1.0], F32), half)
    cos = jnp.concatenate([cos, jnp.ones((pad_rows, HEAD_DIM), F32)], axis=0)
    sin = jnp.concatenate([sin, jnp.zeros((pad_rows, HEAD_DIM), F32)], axis=0)
    return cos, sin


def kernel(x, c, ctx, c_ctx, ada_w, ada_b, w_in, q_norm_g, k_norm_g, conv_w, w_attn_o, w_conv_o, w_mix_o,
           ln1_g, ln1_b, router_w, router_b, w_up, b_up, w_down, b_down, ln2_g, ln2_b):
    batch, seq, d = x.shape
    ctx_len = ctx.shape[1]
    depth = ada_w.shape[0]
    n_q = d // HEAD_DIM
    d_q, d_kv, d_conv = d, n_q // GQA_GROUP * HEAD_DIM, d
    o_k, o_v = d_q, d_q + d_kv
    o_cb = o_v + d_kv
    o_cc, o_cx = o_cb + d_conv, o_cb + 2 * d_conv
    o_ga = o_cx + d_conv
    n_lat, n_ctx = batch * seq, batch * ctx_len
    n_tok = n_lat + n_ctx
    alpha = (2 * depth) ** 0.25
    assert seq & (seq - 1) == 0 and ctx_len & (ctx_len - 1) == 0 and seq % ctx_len == 0 and seq % GRID_W == 0

    tn = next(t for t in (512, 256, 128) if all(o % t == 0 for o in (o_k, o_v, o_cb, o_cc, o_cx, o_ga)))
    row_unit = math.gcd(seq, n_ctx)
    tm_proj, tm_merge, tm_small = _tile(row_unit, 1024), _tile(row_unit, 1024), _tile(row_unit, 256)

    mod_rows = -(-(batch + 1) // 8) * 8
    cvec = jnp.zeros((mod_rows, d), F32).at[:batch].set(c).at[batch].set(c_ctx)
    mod = _adaln(cvec, ada_w, ada_b).reshape(depth, mod_rows, 6, 1, d)

    cos, sin = _rope_tables(seq, tm_proj)
    scale = HEAD_DIM ** -0.5
    state = jnp.concatenate([x.reshape(n_lat, d), ctx.reshape(n_ctx, d)], axis=0)
    n_chunks = d // 2 // LANES
    w_in_b, w_attn_b, w_conv_b, w_mix_b = (w.astype(BF16) for w in (w_in, w_attn_o, w_conv_o, w_mix_o))
    w_up_b, w_down_b = w_up.astype(BF16), w_down.astype(BF16)
    pair = jnp.arange(HEAD_DIM) ^ 1

    for l in range(depth):
        last = l == depth - 1
        sh1, sc1, g1, sh2, sc2, g2 = [mod[l, :, m] for m in range(6)]
        gq, gk = q_norm_g[l] * scale, k_norm_g[l]
        tab_a = jnp.stack([cos * gq, cos * gk])
        tab_b = jnp.stack([sin * gq[pair], sin * gk[pair]])
        n_main = n_lat if last else n_tok

        rows_all = _Rows(n_tok, tm_proj, batch, seq, n_lat)
        rows_main = _Rows(n_main, tm_proj, batch, seq, n_lat)
        qk = _proj("qk", rows_all, state, sc1, sh1, w_in_b, l, [0], d_q + d_kv, tn,
                   extra=(tab_a, tab_b, d_q // tn), name="proj_qk")
        n_vt, n_vcb = d_kv // tn, (d_kv + d_conv) // tn
        vcb = _proj("plain", rows_all, state, sc1, sh1, w_in_b, l, [o_v], d_kv + d_conv, tn,
                    out_map=lambda i, j: (i, (j + n_vcb - n_vt) % n_vcb), name="proj_vcb")
        z = _proj("product", rows_main, state, sc1, sh1, w_in_b, l, [o_cc, o_cx], d_conv, tn, name="proj_z")
        gates = _proj("sigmoid", rows_main, state, sc1, sh1, w_in_b, l, [o_ga], 2 * d, tn, name="proj_gates")

        attn = _attention(qk, vcb, batch, seq, ctx_len, d_q, d_conv, ctx_queries=not last)

        rows_m = _Rows(n_main, tm_merge, batch, seq, n_lat)
        y = _merge(rows_m, attn, vcb, z, gates, conv_w, w_attn_b, w_conv_b, l, seq, ctx_len)

        rows_s = _Rows(n_main, tm_small, batch, seq, n_lat)
        x1, vp, meta, counts = _mix_router(rows_s, y, w_mix_b, l, state, g1, ln1_g[l], ln1_b[l],
                                           sc2, sh2, router_w[l], router_b[l], alpha)
        dest, block_e, n_used, pad_lo, pad_hi, n_slots = _slot_layout(meta, counts, n_main)
        xg = _dispatch(vp, dest, pad_lo, pad_hi, n_slots, tm_small, n_chunks)
        y_slots = _experts(xg, block_e, n_used, w_up_b, b_up, w_down_b, b_down, l)
        state = _combine(rows_s, dest, meta, x1, g2, ln2_g[l], ln2_b[l], y_slots, alpha)

    return state[:n_lat].reshape(batch, seq, d)
```

```python
import functools
import math

import jax
import jax.numpy as jnp
from jax import lax
from jax.experimental import pallas as pl
from jax.experimental.pallas import tpu as pltpu

F32 = jnp.float32
BF16 = jnp.bfloat16
I32 = jnp.int32

LANES = 128
HEAD_DIM = 128
GQA_GROUP = 4
GRID_W = 64
TOP_K = 4
SWIGLU_ALPHA = 1.702
SWIGLU_LIMIT = 7.0
ROPE_THETA = 10000.0
NORM_EPS = 1e-6
VMEM_LIMIT = 56 * 1024 * 1024
EXPERT_ROWS = 256
META_IDX, META_GATE, META_POS = 0, 4, 8
DMA_UNROLL = 8


def _params(sem, vmem=VMEM_LIMIT):
    return pltpu.CompilerParams(dimension_semantics=sem, vmem_limit_bytes=vmem)


def _tile(n, pref):
    if n <= pref:
        return n
    t = pref - pref % 8
    while n % t:
        t -= 8
    return t


def _adaln_kernel(c_ref, w_ref, b_ref, o_ref):
    c = c_ref[...]
    s = (c * jax.nn.sigmoid(c)).astype(BF16)
    o_ref[...] = jnp.dot(s, w_ref[...].astype(BF16), preferred_element_type=F32) + b_ref[...]


def _adaln(cvec, ada_w, ada_b):
    depth, d, n = ada_w.shape
    r = cvec.shape[0]
    tn = _tile(n, 1024)
    return pl.pallas_call(
        _adaln_kernel,
        grid=(depth, n // tn),
        in_specs=[
            pl.BlockSpec((r, d), lambda l, j: (0, 0)),
            pl.BlockSpec((None, d, tn), lambda l, j: (l, 0, j)),
            pl.BlockSpec((None, 1, tn), lambda l, j: (l, 0, j)),
        ],
        out_specs=pl.BlockSpec((None, r, tn), lambda l, j: (l, 0, j)),
        out_shape=jax.ShapeDtypeStruct((depth, r, n), F32),
        compiler_params=_params(("parallel", "parallel")),
        name="adaln",
    )(cvec, ada_w, ada_b.reshape(depth, 1, n))


def _modulate(u_scr, x_ref, sc_ref, sh_ref):
    @pl.when(pl.program_id(1) == 0)
    def _():
        u_scr[...] = (x_ref[...] * (1.0 + sc_ref[...]) + sh_ref[...]).astype(BF16)


def _sigmoid(x):
    return 0.5 * jnp.tanh(0.5 * x) + 0.5


def _norm_rope_store(acc, ta, tb, o_ref):
    tn = acc.shape[1]
    rr = lax.broadcasted_iota(I32, (HEAD_DIM, HEAD_DIM), 0)
    cc = lax.broadcasted_iota(I32, (HEAD_DIM, HEAD_DIM), 1)
    swap = jnp.where((rr ^ 1) == cc, 1.0, 0.0).astype(BF16)
    mean_w = jnp.full((HEAD_DIM, HEAD_DIM), 1.0 / HEAD_DIM, BF16)
    for h in range(tn // HEAD_DIM):
        cols = slice(h * HEAD_DIM, (h + 1) * HEAD_DIM)
        a = acc[:, cols]
        ms = jnp.dot((a * a).astype(BF16), mean_w, preferred_element_type=F32)
        partner = jnp.dot(a.astype(BF16), swap, preferred_element_type=F32)
        o_ref[:, cols] = (lax.rsqrt(ms + NORM_EPS) * (a * ta + partner * tb)).astype(BF16)


def _proj_qk_kernel(x_ref, sc_ref, sh_ref, w_ref, ta_ref, tb_ref, o_ref, u_scr):
    _modulate(u_scr, x_ref, sc_ref, sh_ref)
    acc = jnp.dot(u_scr[...], w_ref[...], preferred_element_type=F32)
    _norm_rope_store(acc, ta_ref[...], tb_ref[...], o_ref)


def _proj_plain_kernel(x_ref, sc_ref, sh_ref, w_ref, o_ref, u_scr):
    _modulate(u_scr, x_ref, sc_ref, sh_ref)
    o_ref[...] = jnp.dot(u_scr[...], w_ref[...], preferred_element_type=F32).astype(BF16)


def _proj_fused_kernel(x_ref, sc_ref, sh_ref, w_ref, wx_ref, ta_ref, tb_ref, qk_ref, vcb_ref, z_ref, g_ref, u_scr,
                       *, n_qk, n_vcb, n_z):
    _modulate(u_scr, x_ref, sc_ref, sh_ref)
    j = pl.program_id(1)
    z0 = n_qk + n_vcb

    @pl.when(j < n_qk)
    def _():
        acc = jnp.dot(u_scr[...], w_ref[...], preferred_element_type=F32)
        _norm_rope_store(acc, ta_ref[...], tb_ref[...], qk_ref)

    @pl.when((j >= n_qk) & (j < z0))
    def _():
        vcb_ref[...] = jnp.dot(u_scr[...], w_ref[...], preferred_element_type=F32).astype(BF16)

    @pl.when((j >= z0) & (j < z0 + n_z))
    def _():
        u = u_scr[...]
        c = jnp.dot(u, w_ref[...], preferred_element_type=F32)
        xin = jnp.dot(u, wx_ref[...], preferred_element_type=F32)
        z_ref[...] = (c * xin).astype(BF16)

    @pl.when(j >= z0 + n_z)
    def _():
        acc = jnp.dot(u_scr[...], w_ref[...], preferred_element_type=F32)
        g_ref[...] = _sigmoid(acc).astype(BF16)


class _Rows:
    def __init__(self, n_rows, tm, batch, seq, n_lat, tile0=0):
        assert n_rows % tm == 0 and seq % tm == 0 and n_lat % tm == 0
        self.n_rows, self.tm, self.batch, self.tile0 = n_rows, tm, batch, tile0
        self.n_tiles = n_rows // tm
        self.lat_tiles = n_lat // tm
        self.per_seq = seq // tm

    def state_tile(self, i):
        return i + self.tile0

    def mod_row(self, i):
        t = i + self.tile0
        return jnp.where(t < self.lat_tiles, t // self.per_seq, self.batch)

    def rope_blk(self, i):
        t = i + self.tile0
        return jnp.where(t < self.lat_tiles, t % self.per_seq, self.per_seq)


def _proj_common_specs(rows, d):
    return [
        pl.BlockSpec((rows.tm, d), lambda i, j: (rows.state_tile(i), 0)),
        pl.BlockSpec((None, 1, d), lambda i, j: (rows.mod_row(i), 0, 0)),
        pl.BlockSpec((None, 1, d), lambda i, j: (rows.mod_row(i), 0, 0)),
    ]


def _proj(kind, rows, x, sc, sh, w, layer, col_off, n_cols, tn, extra=(), out_map=None, name="proj"):
    d = x.shape[1]
    tm = rows.tm
    assert col_off % tn == 0
    off = col_off // tn
    in_specs = _proj_common_specs(rows, d) + [pl.BlockSpec((None, d, tn), lambda i, j: (layer, 0, off + j))]
    operands = [x, sc, sh, w]
    if kind == "qk":
        tab_a, tab_b, q_tiles = extra
        tab_spec = pl.BlockSpec((None, tm, HEAD_DIM), lambda i, j: (jnp.where(j < q_tiles, 0, 1), rows.rope_blk(i), 0))
        in_specs += [tab_spec, tab_spec]
        operands += [tab_a, tab_b]
    if out_map is None:
        out_map = lambda i, j: (i, j)
    return pl.pallas_call(
        {"qk": _proj_qk_kernel, "plain": _proj_plain_kernel}[kind],
        grid=(rows.n_tiles, n_cols // tn),
        in_specs=in_specs,
        out_specs=pl.BlockSpec((tm, tn), out_map),
        out_shape=jax.ShapeDtypeStruct((rows.n_rows, n_cols), BF16),
        scratch_shapes=[pltpu.VMEM((tm, d), BF16)],
        compiler_params=_params(("parallel", "arbitrary")),
        name=name,
    )(*operands)


def _proj_fused(rows, x, sc, sh, w, layer, tab_a, tab_b, d_q, d_kv, d_conv, tn):
    d = x.shape[1]
    tm = rows.tm
    n_qk, n_vt, n_vcb, n_z, n_g = (d_q + d_kv) // tn, d_kv // tn, (d_kv + d_conv) // tn, d_conv // tn, 2 * d // tn
    q_tiles = d_q // tn
    z0 = n_qk + n_vcb

    def clamp(v, n):
        return jnp.clip(v, 0, n - 1)

    tab_spec = pl.BlockSpec((None, tm, HEAD_DIM), lambda i, j: (jnp.where(j < q_tiles, 0, 1), rows.rope_blk(i), 0))
    in_specs = _proj_common_specs(rows, d) + [
        pl.BlockSpec((None, d, tn), lambda i, j: (layer, 0, j + jnp.where(j >= z0 + n_z, n_z, 0))),
        pl.BlockSpec((None, d, tn), lambda i, j: (layer, 0, z0 + n_z + clamp(j - z0, n_z))),
        tab_spec, tab_spec,
    ]
    out_specs = [
        pl.BlockSpec((tm, tn), lambda i, j: (i, clamp(j, n_qk))),
        pl.BlockSpec((tm, tn), lambda i, j: (i, (clamp(j - n_qk, n_vcb) + n_vcb - n_vt) % n_vcb)),
        pl.BlockSpec((tm, tn), lambda i, j: (i, clamp(j - z0, n_z))),
        pl.BlockSpec((tm, tn), lambda i, j: (i, clamp(j - z0 - n_z, n_g))),
    ]
    out_shape = [jax.ShapeDtypeStruct((rows.n_rows, n * tn), BF16) for n in (n_qk, n_vcb, n_z, n_g)]
    return pl.pallas_call(
        functools.partial(_proj_fused_kernel, n_qk=n_qk, n_vcb=n_vcb, n_z=n_z),
        grid=(rows.n_tiles, n_qk + n_vcb + n_z + n_g),
        in_specs=in_specs,
        out_specs=out_specs,
        out_shape=out_shape,
        scratch_shapes=[pltpu.VMEM((tm, d), BF16)],
        compiler_params=_params(("parallel", "arbitrary")),
        name="proj_fused",
    )(x, sc, sh, w, w, tab_a, tab_b)


def _attn_heads(q_ref, kl_ref, kc_ref, vl_ext, vc_ext, o_ref, with_latent):
    nt = (((1,), (1,)), ((), ()))
    for h in range(GQA_GROUP):
        cols = slice(h * HEAD_DIM, (h + 1) * HEAD_DIM)
        q = q_ref[:, cols]
        s_c = lax.dot_general(q, kc_ref[...], nt, preferred_element_type=F32)
        m = jnp.max(s_c, axis=-1, keepdims=True)
        if with_latent:
            s_l = lax.dot_general(q, kl_ref[...], nt, preferred_element_type=F32)
            m = jnp.maximum(m, jnp.max(s_l, axis=-1, keepdims=True))
        p_c = jnp.exp((s_c - m).astype(BF16))
        o = jnp.dot(p_c, vc_ext[...], preferred_element_type=F32)
        if with_latent:
            p_l = jnp.exp((s_l - m).astype(BF16))
            o = o + jnp.dot(p_l, vl_ext[...], preferred_element_type=F32)
        o_ref[:, cols] = (o[:, :HEAD_DIM] / o[:, HEAD_DIM:]).astype(BF16)


def _attn_kernel(q_ref, kl_ref, vl_ref, kc_ref, vc_ref, o_ref, vl_ext, vc_ext, *, lat_steps, ctx_step):
    qi = pl.program_id(2)

    @pl.when(qi == 0)
    def _():
        vl_ext[:, :HEAD_DIM] = vl_ref[...]
        vl_ext[:, HEAD_DIM:] = jnp.ones_like(vl_ref)
        vc_ext[:, :HEAD_DIM] = vc_ref[...]
        vc_ext[:, HEAD_DIM:] = jnp.ones_like(vc_ref)

    if not ctx_step:
        _attn_heads(q_ref, kl_ref, kc_ref, vl_ext, vc_ext, o_ref, True)
        return

    @pl.when(qi < lat_steps)
    def _():
        _attn_heads(q_ref, kl_ref, kc_ref, vl_ext, vc_ext, o_ref, True)

    @pl.when(qi >= lat_steps)
    def _():
        _attn_heads(q_ref, kl_ref, kc_ref, vl_ext, vc_ext, o_ref, False)


def _attention(qk, vcb, qk_ctx, vcb_ctx, ctx_blk0, batch, seq, ctx_len, d_q, d_conv, ctx_queries):
    n_kv = d_q // HEAD_DIM // GQA_GROUP
    tq = ctx_len
    lat_steps = seq // tq
    gw = GQA_GROUP * HEAD_DIM
    k_col = d_q // HEAD_DIM
    v_col = d_conv // HEAD_DIM
    n_rows = batch * seq + (batch * ctx_len if ctx_queries else 0)

    def q_map(b, g, qi):
        return (jnp.where(qi < lat_steps, b * lat_steps + qi, batch * lat_steps + b), g)

    return pl.pallas_call(
        functools.partial(_attn_kernel, lat_steps=lat_steps, ctx_step=ctx_queries),
        grid=(batch, n_kv, lat_steps + (1 if ctx_queries else 0)),
        in_specs=[
            pl.BlockSpec((tq, gw), q_map),
            pl.BlockSpec((seq, HEAD_DIM), lambda b, g, qi: (b, k_col + g)),
            pl.BlockSpec((seq, HEAD_DIM), lambda b, g, qi: (b, v_col + g)),
            pl.BlockSpec((ctx_len, HEAD_DIM), lambda b, g, qi: (ctx_blk0 + b, k_col + g)),
            pl.BlockSpec((ctx_len, HEAD_DIM), lambda b, g, qi: (ctx_blk0 + b, v_col + g)),
        ],
        out_specs=pl.BlockSpec((tq, gw), q_map),
        out_shape=jax.ShapeDtypeStruct((n_rows, d_q), BF16),
        scratch_shapes=[pltpu.VMEM((seq, 2 * HEAD_DIM), BF16), pltpu.VMEM((ctx_len, 2 * HEAD_DIM), BF16)],
        compiler_params=_params(("parallel", "parallel", "arbitrary")),
        name="attention",
    )(qk, qk, vcb, qk_ctx, vcb_ctx)


def _merge_kernel(attn_ref, cb_ref, z_ref, zprev_ref, znext_ref, cw_ref, wa_ref, wc_ref, ga_ref, gc_ref,
                  o_ref, conv_scr, *, rows, seq, ctx_len, chunk):
    i = pl.program_id(0)

    @pl.when(pl.program_id(1) == 0)
    def _():
        tm, d = conv_scr.shape
        seq_len = jnp.where(i < rows.lat_tiles, seq, ctx_len)
        r = lax.broadcasted_iota(I32, (tm, chunk), 0)
        pos = (r + i * tm) & (seq_len - 1)
        first, last = pos == 0, pos == seq_len - 1
        for c0 in range(0, d, chunk):
            cols = slice(c0, c0 + chunk)
            z = z_ref[:, cols].astype(F32)
            z_prev = zprev_ref[:, cols].astype(F32)
            z_next = znext_ref[:, cols].astype(F32)
            before = jnp.where(r == 0, z_prev[7:8], pltpu.roll(z, 1, 0))
            after = jnp.where(r == tm - 1, z_next[0:1], pltpu.roll(z, tm - 1, 0))
            before = jnp.where(first, 0.0, before)
            after = jnp.where(last, 0.0, after)
            conv = cw_ref[0:1, cols] * before + cw_ref[1:2, cols] * z + cw_ref[2:3, cols] * after
            conv_scr[:, cols] = (cb_ref[:, cols].astype(F32) * conv).astype(BF16)

    a = jnp.dot(attn_ref[...], wa_ref[...], preferred_element_type=F32)
    c = jnp.dot(conv_scr[...], wc_ref[...], preferred_element_type=F32)
    o_ref[...] = (ga_ref[...].astype(F32) * a + gc_ref[...].astype(F32) * c).astype(BF16)


def _merge(rows, attn, vcb, z, gates, conv_w, w_attn_o, w_conv_o, layer, seq, ctx_len):
    d = attn.shape[1]
    tm = rows.tm
    tn = _tile(d, 512)
    n_j = d // tn
    halo = tm // 8
    last_halo = rows.n_rows // 8 - 1
    return pl.pallas_call(
        functools.partial(_merge_kernel, rows=rows, seq=seq, ctx_len=ctx_len, chunk=_tile(d, 512)),
        grid=(rows.n_tiles, n_j),
        in_specs=[
            pl.BlockSpec((tm, d), lambda i, j: (i, 0)),
            pl.BlockSpec((tm, d), lambda i, j: (i, 0)),
            pl.BlockSpec((tm, d), lambda i, j: (i, 0)),
            pl.BlockSpec((8, d), lambda i, j: (jnp.maximum(i * halo - 1, 0), 0)),
            pl.BlockSpec((8, d), lambda i, j: (jnp.minimum((i + 1) * halo, last_halo), 0)),
            pl.BlockSpec((None, 3, d), lambda i, j: (layer, 0, 0)),
            pl.BlockSpec((None, d, tn), lambda i, j: (layer, 0, j)),
            pl.BlockSpec((None, d, tn), lambda i, j: (layer, 0, j)),
            pl.BlockSpec((tm, tn), lambda i, j: (i, j)),
            pl.BlockSpec((tm, tn), lambda i, j: (i, n_j + j)),
        ],
        out_specs=pl.BlockSpec((tm, tn), lambda i, j: (i, j)),
        out_shape=jax.ShapeDtypeStruct((rows.n_rows, d), BF16),
        scratch_shapes=[pltpu.VMEM((tm, d), BF16)],
        compiler_params=_params(("parallel", "arbitrary")),
        name="merge",
    )(attn, vcb, z, z, z, conv_w, w_attn_o, w_conv_o, gates, gates)


def _layer_norm(r, g, b):
    mu = jnp.mean(r, axis=-1, keepdims=True)
    c = r - mu
    var = jnp.mean(c * c, axis=-1, keepdims=True)
    return c * lax.rsqrt(var + NORM_EPS) * g + b


def _pack_bf16_pair(lo, hi):
    lo_bits = lax.bitcast_convert_type(lo.astype(BF16).astype(F32), I32)
    hi_bits = lax.bitcast_convert_type(hi.astype(BF16).astype(F32), I32)
    return lax.shift_right_logical(lo_bits, jnp.full_like(lo_bits, 16)) | (hi_bits & jnp.int32(-65536))


def _unpack_bf16_pair(w):
    lo = lax.bitcast_convert_type(lax.shift_left(w, jnp.full_like(w, 16)), F32)
    hi = lax.bitcast_convert_type(w & jnp.int32(-65536), F32)
    return lo, hi


def _store_token_rows(ref, packed):
    tm, half = packed.shape
    n_chunks = half // LANES
    for c in range(n_chunks):
        ref[pl.ds(c, tm, stride=n_chunks), :] = packed[:, c * LANES:(c + 1) * LANES]


def _load_token_chunk(ref, lead, c, tm, n_chunks):
    return ref[(*lead, pl.ds(c, tm, stride=n_chunks), slice(None))]


def _mix_router_kernel(y_ref, wmix_ref, x_ref, g1_ref, lng_ref, lnb_ref, sc2_ref, sh2_ref, rw2_ref, rwh_ref, rb_ref,
                       x1_ref, vp_ref, meta_ref, cnt_ref, run_scr, *, alpha, n_experts):
    @pl.when(pl.program_id(0) == 0)
    def _():
        run_scr[...] = jnp.zeros_like(run_scr)

    mix = jnp.dot(y_ref[...], wmix_ref[...], preferred_element_type=F32)
    x1 = _layer_norm(alpha * x_ref[...] + g1_ref[...] * mix, lng_ref[...], lnb_ref[...])
    x1_ref[...] = x1
    v = x1 * (1.0 + sc2_ref[...]) + sh2_ref[...]
    tm, d = v.shape
    _store_token_rows(vp_ref, _pack_bf16_pair(v[:, : d // 2], v[:, d // 2:]))

    v_hi = v.astype(BF16)
    v_lo = (v - v_hi.astype(F32)).astype(BF16)
    hh_hl = jnp.dot(v_hi, rw2_ref[...], preferred_element_type=F32)
    lh = jnp.dot(v_lo, rwh_ref[...], preferred_element_type=F32)
    logits = hh_hl[:, :n_experts] + hh_hl[:, n_experts:] + lh + rb_ref[...]
    lane = lax.broadcasted_iota(I32, (tm, n_experts), 1).astype(F32)
    work = logits
    sel, val = [], []
    for _ in range(TOP_K):
        m = jnp.max(work, axis=-1, keepdims=True)
        s = jnp.min(jnp.where(work == m, lane, float(n_experts)), axis=-1, keepdims=True)
        sel.append(s)
        val.append(m)
        work = jnp.where(lane == s, -jnp.inf, work)
    ex = [jnp.exp(v_k - val[0]) for v_k in val]
    den = ex[0] + ex[1] + ex[2] + ex[3]

    onehot = jnp.zeros((tm, n_experts), F32)
    for s in sel:
        onehot = onehot + jnp.where(lane == s, 1.0, 0.0)
    rr = lax.broadcasted_iota(I32, (tm, tm), 0)
    cc = lax.broadcasted_iota(I32, (tm, tm), 1)
    tri = jnp.where(cc <= rr, 1.0, 0.0).astype(BF16)
    incl = jnp.dot(tri, onehot.astype(BF16), preferred_element_type=F32)
    before = run_scr[...] + incl - onehot

    mlane = lax.broadcasted_iota(I32, (tm, LANES), 1)
    meta = jnp.zeros((tm, LANES), F32)
    for k in range(TOP_K):
        pos_k = jnp.sum(jnp.where(lane == sel[k], before, 0.0), axis=-1, keepdims=True)
        meta = jnp.where(mlane == META_IDX + k, sel[k], meta)
        meta = jnp.where(mlane == META_GATE + k, ex[k] / den, meta)
        meta = jnp.where(mlane == META_POS + k, pos_k, meta)
    meta_ref[...] = meta
    run_scr[...] = run_scr[...] + incl[tm - 1:tm, :]
    cnt_ref[...] = run_scr[...]


def _mix_router(rows, y, w_mix, layer, x, g1, ln_g, ln_b, sc2, sh2, router_w, router_b, alpha):
    d = x.shape[1]
    tm = rows.tm
    e = router_w.shape[1]
    n_chunks = d // 2 // LANES
    rw_hi = router_w.astype(BF16)
    rw_lo = (router_w - rw_hi.astype(F32)).astype(BF16)
    mod_spec = pl.BlockSpec((None, 1, d), lambda i: (rows.mod_row(i), 0, 0))
    vec_spec = pl.BlockSpec((1, d), lambda i: (0, 0))
    return pl.pallas_call(
        functools.partial(_mix_router_kernel, alpha=alpha, n_experts=e),
        grid=(rows.n_tiles,),
        in_specs=[
            pl.BlockSpec((tm, d), lambda i: (i, 0)),
            pl.BlockSpec((None, d, d), lambda i: (layer, 0, 0)),
            pl.BlockSpec((tm, d), lambda i: (i, 0)),
            mod_spec, vec_spec, vec_spec, mod_spec, mod_spec,
            pl.BlockSpec((d, 2 * e), lambda i: (0, 0)),
            pl.BlockSpec((d, e), lambda i: (0, 0)),
            pl.BlockSpec((1, e), lambda i: (0, 0)),
        ],
        out_specs=[
            pl.BlockSpec((tm, d), lambda i: (i, 0)),
            pl.BlockSpec((tm * n_chunks, LANES), lambda i: (i, 0)),
            pl.BlockSpec((tm, LANES), lambda i: (i, 0)),
            pl.BlockSpec((1, e), lambda i: (0, 0)),
        ],
        out_shape=[
            jax.ShapeDtypeStruct((rows.n_rows, d), F32),
            jax.ShapeDtypeStruct((rows.n_rows * n_chunks, LANES), I32),
            jax.ShapeDtypeStruct((rows.n_rows, LANES), F32),
            jax.ShapeDtypeStruct((1, e), F32),
        ],
        scratch_shapes=[pltpu.VMEM((1, e), F32)],
        compiler_params=_params(("arbitrary",)),
        name="mix_ln_router",
    )(y, w_mix, x, g1, ln_g.reshape(1, d), ln_b.reshape(1, d), sc2, sh2,
      jnp.concatenate([rw_hi, rw_lo], axis=1), rw_hi, router_b.reshape(1, e))


def _dispatch_kernel(pad_lo_ref, pad_hi_ref, dest_ref, vp_ref, xg_ref, zero_scr, sem, zero_sem, *, tm, n_chunks):
    def slot_rows(ref, s):
        return ref.at[pl.ds(pl.multiple_of(s * n_chunks, n_chunks), n_chunks), :]

    @pl.when(pl.program_id(0) == 0)
    def _():
        zero_scr[...] = jnp.zeros_like(zero_scr)

        def zero_copy(s):
            return pltpu.make_async_copy(zero_scr, slot_rows(xg_ref, s), zero_sem)

        def start(s, carry):
            zero_copy(s).start()
            return carry

        def wait(s, carry):
            zero_copy(s).wait()
            return carry

        for e in range(pad_lo_ref.shape[0]):
            lax.fori_loop(pad_lo_ref[e], pad_hi_ref[e], start, 0)
        for e in range(pad_lo_ref.shape[0]):
            lax.fori_loop(pad_lo_ref[e], pad_hi_ref[e], wait, 0)

    def row_copy(r, d):
        return pltpu.make_async_copy(slot_rows(vp_ref, r), slot_rows(xg_ref, d), sem)

    def issue(r, carry):
        for k in range(TOP_K):
            row_copy(r, dest_ref[r * TOP_K + k]).start(priority=k % 2)
        return carry

    lax.fori_loop(0, tm, issue, 0, unroll=DMA_UNROLL)

    def drain(r, carry):
        row_copy(0, 0).wait()
        return carry

    lax.fori_loop(0, tm * TOP_K, drain, 0, unroll=DMA_UNROLL)


def _dispatch(vp, dest_flat, pad_lo, pad_hi, n_slots, tm, n_chunks):
    n_rows = vp.shape[0] // n_chunks
    grid_spec = pltpu.PrefetchScalarGridSpec(
        num_scalar_prefetch=2,
        grid=(n_rows // tm,),
        in_specs=[
            pl.BlockSpec((tm * TOP_K,), lambda i, lo, hi: (i,), memory_space=pltpu.SMEM),
            pl.BlockSpec((tm * n_chunks, LANES), lambda i, lo, hi: (i, 0)),
        ],
        out_specs=pl.BlockSpec(memory_space=pl.ANY),
        scratch_shapes=[pltpu.VMEM((n_chunks, LANES), I32), pltpu.SemaphoreType.DMA(()), pltpu.SemaphoreType.DMA(())],
    )
    return pl.pallas_call(
        functools.partial(_dispatch_kernel, tm=tm, n_chunks=n_chunks),
        grid_spec=grid_spec,
        out_shape=jax.ShapeDtypeStruct((n_slots * n_chunks, LANES), I32),
        compiler_params=_params(("arbitrary",)),
        name="dispatch",
    )(pad_lo, pad_hi, dest_flat, vp)


def _expert_kernel(be_ref, nused_ref, xg_ref, wup_ref, bup_ref, wdn_ref, bdn_ref, y_ref, x_scr):
    del be_ref
    i = pl.program_id(0)

    @pl.when(i < nused_ref[0])
    def _():
        bm, d = x_scr.shape
        half = d // 2
        n_chunks = half // LANES
        for c in range(n_chunks):
            lo, hi = _unpack_bf16_pair(_load_token_chunk(xg_ref, (), c, bm, n_chunks))
            x_scr[:, c * LANES:(c + 1) * LANES] = lo.astype(BF16)
            x_scr[:, half + c * LANES:half + (c + 1) * LANES] = hi.astype(BF16)
        h = jnp.dot(x_scr[...], wup_ref[...], preferred_element_type=F32) + bup_ref[...]
        f = h.shape[1] // 2
        glu = jnp.minimum(h[:, :f], SWIGLU_LIMIT)
        lin = jnp.clip(h[:, f:], -SWIGLU_LIMIT, SWIGLU_LIMIT)
        act = glu * _sigmoid(SWIGLU_ALPHA * glu) * (lin + 1.0)
        y = jnp.dot(act.astype(BF16), wdn_ref[...], preferred_element_type=F32) + bdn_ref[...]
        _store_token_rows(y_ref, _pack_bf16_pair(y[:, :half], y[:, half:]))

    @pl.when(i >= nused_ref[0])
    def _():
        y_ref[...] = jnp.zeros_like(y_ref)


def _experts(xg, block_e, n_used, w_up, b_up, w_down, b_down, layer):
    _, e, d, f2 = w_up.shape
    n_chunks = d // 2 // LANES
    bm = EXPERT_ROWS
    n_blocks = xg.shape[0] // n_chunks // bm

    def x_map(i, be, nu):
        return (jnp.maximum(jnp.minimum(i, nu[0] - 1), 0), 0)

    def w_map(i, be, nu):
        return (layer, be[i], 0, 0)

    grid_spec = pltpu.PrefetchScalarGridSpec(
        num_scalar_prefetch=2,
        grid=(n_blocks,),
        in_specs=[
            pl.BlockSpec((bm * n_chunks, LANES), x_map),
            pl.BlockSpec((None, None, d, f2), w_map),
            pl.BlockSpec((None, None, 1, f2), w_map),
            pl.BlockSpec((None, None, f2 // 2, d), w_map),
            pl.BlockSpec((None, None, 1, d), w_map),
        ],
        out_specs=pl.BlockSpec((bm * n_chunks, LANES), lambda i, be, nu: (i, 0)),
        scratch_shapes=[pltpu.VMEM((bm, d), BF16)],
    )
    depth = w_up.shape[0]
    return pl.pallas_call(
        _expert_kernel,
        grid_spec=grid_spec,
        out_shape=jax.ShapeDtypeStruct(xg.shape, I32),
        compiler_params=_params(("arbitrary",)),
        name="experts",
    )(block_e, n_used, xg, w_up, b_up.reshape(depth, e, 1, f2), w_down, b_down.reshape(depth, e, 1, d))


def _combine_kernel(dcur_ref, dnext_ref, meta_ref, x1_ref, g2_ref, lng_ref, lnb_ref, y_hbm,
                    o_ref, buf, f_scr, sem, *, alpha):
    i = pl.program_id(0)
    n = pl.num_programs(0)
    tm, d = x1_ref.shape
    half = d // 2
    n_chunks = half // LANES

    def row_copy(d_slot, slot, r, k):
        src = y_hbm.at[pl.ds(pl.multiple_of(d_slot * n_chunks, n_chunks), n_chunks), :]
        dst = buf.at[slot, k, pl.ds(pl.multiple_of(r * n_chunks, n_chunks), n_chunks), :]
        return pltpu.make_async_copy(src, dst, sem.at[slot])

    def issue_tile(dref, slot):
        def body(r, carry):
            for k in range(TOP_K):
                row_copy(dref[r * TOP_K + k], slot, r, k).start(priority=k % 2)
            return carry
        lax.fori_loop(0, tm, body, 0, unroll=DMA_UNROLL)

    @pl.when(i == 0)
    def _():
        issue_tile(dcur_ref, 0)

    @pl.when(i + 1 < n)
    def _():
        issue_tile(dnext_ref, (i + 1) % 2)

    slot = i % 2

    def drain(r, carry):
        row_copy(0, slot, 0, 0).wait()
        return carry

    lax.fori_loop(0, tm * TOP_K, drain, 0, unroll=DMA_UNROLL)

    meta = meta_ref[...]
    gate = [meta[:, META_GATE + k:META_GATE + k + 1] for k in range(TOP_K)]
    for c in range(n_chunks):
        f_lo = jnp.zeros((tm, LANES), F32)
        f_hi = jnp.zeros((tm, LANES), F32)
        for k in range(TOP_K):
            lo, hi = _unpack_bf16_pair(_load_token_chunk(buf, (slot, k), c, tm, n_chunks))
            f_lo = f_lo + gate[k] * lo
            f_hi = f_hi + gate[k] * hi
        f_scr[:, c * LANES:(c + 1) * LANES] = f_lo
        f_scr[:, half + c * LANES:half + (c + 1) * LANES] = f_hi
    o_ref[...] = _layer_norm(alpha * x1_ref[...] + g2_ref[...] * f_scr[...], lng_ref[...], lnb_ref[...])


def _combine(rows, dest_flat, meta, x1, g2, ln_g, ln_b, y_slots, alpha):
    d = x1.shape[1]
    tm = rows.tm
    n_chunks = d // 2 // LANES
    last = rows.n_tiles - 1
    vec_spec = pl.BlockSpec((1, d), lambda i: (0, 0))
    return pl.pallas_call(
        functools.partial(_combine_kernel, alpha=alpha),
        grid=(rows.n_tiles,),
        in_specs=[
            pl.BlockSpec((tm * TOP_K,), lambda i: (i,), memory_space=pltpu.SMEM),
            pl.BlockSpec((tm * TOP_K,), lambda i: (jnp.minimum(i + 1, last),), memory_space=pltpu.SMEM),
            pl.BlockSpec((tm, LANES), lambda i: (i, 0)),
            pl.BlockSpec((tm, d), lambda i: (i, 0)),
            pl.BlockSpec((None, 1, d), lambda i: (rows.mod_row(i), 0, 0)),
            vec_spec, vec_spec,
            pl.BlockSpec(memory_space=pl.ANY),
        ],
        out_specs=pl.BlockSpec((tm, d), lambda i: (i, 0)),
        out_shape=jax.ShapeDtypeStruct((rows.n_rows, d), F32),
        scratch_shapes=[pltpu.VMEM((2, TOP_K, tm * n_chunks, LANES), I32), pltpu.VMEM((tm, d), F32),
                        pltpu.SemaphoreType.DMA((2,))],
        compiler_params=_params(("arbitrary",)),
        name="combine_ln",
    )(dest_flat, dest_flat, meta, x1, g2, ln_g.reshape(1, d), ln_b.reshape(1, d), y_slots)


def _slot_layout(meta, counts, n_rows):
    e = counts.shape[1]
    bm = EXPERT_ROWS
    cnt = counts[0].astype(I32)
    padded = (cnt + bm - 1) // bm * bm
    pend = jnp.cumsum(padded)
    pstart = pend - padded
    idx = meta[:, META_IDX:META_IDX + TOP_K].astype(I32)
    pos = meta[:, META_POS:META_POS + TOP_K].astype(I32)
    onehot = idx[..., None] == jnp.arange(e, dtype=I32)
    dest = jnp.sum(jnp.where(onehot, pstart, 0), axis=-1) + pos
    n_blocks = -(-(n_rows * TOP_K + e * (bm - 1)) // bm)
    n_used = pend[-1] // bm
    blk = jnp.minimum(jnp.arange(n_blocks, dtype=I32), n_used - 1) * bm
    block_e = jnp.minimum(jnp.sum(pend[None, :] <= blk[:, None], axis=-1), e - 1).astype(I32)
    pad_lo = jnp.concatenate([pstart + cnt, pend[-1:]]).astype(I32)
    pad_hi = jnp.concatenate([pend, jnp.full((1,), n_blocks * bm, I32)]).astype(I32)
    return dest.reshape(-1), block_e, n_used.reshape(1).astype(I32), pad_lo, pad_hi, n_blocks * bm


def _rope_tables(seq, pad_rows):
    rows = seq // GRID_W
    row = jnp.repeat(jnp.arange(rows, dtype=I32), GRID_W).astype(F32)
    col = jnp.tile(jnp.arange(GRID_W, dtype=I32), rows).astype(F32)
    half = HEAD_DIM // 2
    inv = ROPE_THETA ** (-jnp.arange(0, half, 2, dtype=F32) / half)
    ang = jnp.concatenate([row[:, None] * inv, col[:, None] * inv], axis=-1)
    cos = jnp.repeat(jnp.cos(ang), 2, axis=-1)
    sin = jnp.repeat(jnp.sin(ang), 2, axis=-1) * jnp.tile(jnp.array([-1.0, 1.0], F32), half)
    cos = jnp.concatenate([cos, jnp.ones((pad_rows, HEAD_DIM), F32)], axis=0)
    sin = jnp.concatenate([sin, jnp.zeros((pad_rows, HEAD_DIM), F32)], axis=0)
    return cos, sin


def kernel(x, c, ctx, c_ctx, ada_w, ada_b, w_in, q_norm_g, k_norm_g, conv_w, w_attn_o, w_conv_o, w_mix_o,
           ln1_g, ln1_b, router_w, router_b, w_up, b_up, w_down, b_down, ln2_g, ln2_b):
    batch, seq, d = x.shape
    ctx_len = ctx.shape[1]
    depth = ada_w.shape[0]
    n_q = d // HEAD_DIM
    d_q, d_kv, d_conv = d, n_q // GQA_GROUP * HEAD_DIM, d
    o_k, o_v = d_q, d_q + d_kv
    o_cb = o_v + d_kv
    o_cc, o_cx = o_cb + d_conv, o_cb + 2 * d_conv
    o_ga = o_cx + d_conv
    n_lat, n_ctx = batch * seq, batch * ctx_len
    n_tok = n_lat + n_ctx
    alpha = (2 * depth) ** 0.25
    assert seq & (seq - 1) == 0 and ctx_len & (ctx_len - 1) == 0 and seq % ctx_len == 0 and seq % GRID_W == 0

    tn = next(t for t in (512, 256, 128) if all(o % t == 0 for o in (o_k, o_v, o_cb, o_cc, o_cx, o_ga)))
    row_unit = math.gcd(seq, n_ctx)
    tm_proj, tm_merge, tm_small = _tile(row_unit, 1024), _tile(row_unit, 1024), _tile(row_unit, 256)

    mod_rows = -(-(batch + 1) // 8) * 8
    cvec = jnp.zeros((mod_rows, d), F32).at[:batch].set(c).at[batch].set(c_ctx)
    mod = _adaln(cvec, ada_w, ada_b).reshape(depth, mod_rows, 6, 1, d)

    cos, sin = _rope_tables(seq, tm_proj)
    scale = HEAD_DIM ** -0.5
    state = jnp.concatenate([x.reshape(n_lat, d), ctx.reshape(n_ctx, d)], axis=0)
    n_chunks = d // 2 // LANES
    w_in_b, w_attn_b, w_conv_b, w_mix_b = (w.astype(BF16) for w in (w_in, w_attn_o, w_conv_o, w_mix_o))
    w_up_b, w_down_b = w_up.astype(BF16), w_down.astype(BF16)
    pair = jnp.arange(HEAD_DIM) ^ 1
    n_vt, n_vcb = d_kv // tn, (d_kv + d_conv) // tn

    for l in range(depth):
        last = l == depth - 1
        sh1, sc1, g1, sh2, sc2, g2 = [mod[l, :, m] for m in range(6)]
        gq, gk = q_norm_g[l] * scale, k_norm_g[l]
        tab_a = jnp.stack([cos * gq, cos * gk])
        tab_b = jnp.stack([sin * gq[pair], sin * gk[pair]])
        n_main = n_lat if last else n_tok

        rows_main = _Rows(n_main, tm_proj, batch, seq, n_lat)
        qk, vcb, z, gates = _proj_fused(rows_main, state, sc1, sh1, w_in_b, l, tab_a, tab_b, d_q, d_kv, d_conv, tn)
        if last:
            rows_ctx = _Rows(n_ctx, tm_proj, batch, seq, n_lat, tile0=n_lat // tm_proj)
            qk_ctx = _proj("qk", rows_ctx, state, sc1, sh1, w_in_b, l, 0, d_q + d_kv, tn,
                           extra=(tab_a, tab_b, d_q // tn), name="proj_qk_ctx")
            vcb_ctx = _proj("plain", rows_ctx, state, sc1, sh1, w_in_b, l, o_v, d_kv + d_conv, tn,
                            out_map=lambda i, j: (i, (j + n_vcb - n_vt) % n_vcb), name="proj_vcb_ctx")
            ctx_blk0 = 0
        else:
            qk_ctx, vcb_ctx, ctx_blk0 = qk, vcb, n_lat // ctx_len

        attn = _attention(qk, vcb, qk_ctx, vcb_ctx, ctx_blk0, batch, seq, ctx_len, d_q, d_conv, ctx_queries=not last)

        rows_m = _Rows(n_main, tm_merge, batch, seq, n_lat)
        y = _merge(rows_m, attn, vcb, z, gates, conv_w, w_attn_b, w_conv_b, l, seq, ctx_len)

        rows_s = _Rows(n_main, tm_small, batch, seq, n_lat)
        x1, vp, meta, counts = _mix_router(rows_s, y, w_mix_b, l, state, g1, ln1_g[l], ln1_b[l],
                                           sc2, sh2, router_w[l], router_b[l], alpha)
        dest, block_e, n_used, pad_lo, pad_hi, n_slots = _slot_layout(meta, counts, n_main)
        xg = _dispatch(vp, dest, pad_lo, pad_hi, n_slots, tm_small, n_chunks)
        y_slots = _experts(xg, block_e, n_used, w_up_b, b_up, w_down_b, b_down, l)
        state = _combine(rows_s, dest, meta, x1, g2, ln2_g[l], ln2_b[l], y_slots, alpha)

    return state[:n_lat].reshape(batch, seq, d)
```

```python
import functools
import math

import jax
import jax.numpy as jnp
from jax import lax
from jax.experimental import pallas as pl
from jax.experimental.pallas import tpu as pltpu

F32 = jnp.float32
BF16 = jnp.bfloat16
I32 = jnp.int32

LANES = 128
HEAD_DIM = 128
GQA_GROUP = 4
GRID_W = 64
TOP_K = 4
SWIGLU_ALPHA = 1.702
SWIGLU_LIMIT = 7.0
ROPE_THETA = 10000.0
NORM_EPS = 1e-6
VMEM_LIMIT = 56 * 1024 * 1024
EXPERT_ROWS = 256
META_IDX, META_GATE, META_POS = 0, 4, 8
DMA_UNROLL = 8


def _params(sem, vmem=VMEM_LIMIT):
    return pltpu.CompilerParams(dimension_semantics=sem, vmem_limit_bytes=vmem)


def _tile(n, pref):
    if n <= pref:
        return n
    t = pref - pref % 8
    while n % t:
        t -= 8
    return t


def _adaln_kernel(c_ref, w_ref, b_ref, o_ref):
    c = c_ref[...]
    s = (c * jax.nn.sigmoid(c)).astype(BF16)
    o_ref[...] = jnp.dot(s, w_ref[...].astype(BF16), preferred_element_type=F32) + b_ref[...]


def _adaln(cvec, ada_w, ada_b):
    depth, d, n = ada_w.shape
    r = cvec.shape[0]
    tn = _tile(n, 1024)
    return pl.pallas_call(
        _adaln_kernel,
        grid=(depth, n // tn),
        in_specs=[
            pl.BlockSpec((r, d), lambda l, j: (0, 0)),
            pl.BlockSpec((None, d, tn), lambda l, j: (l, 0, j)),
            pl.BlockSpec((None, 1, tn), lambda l, j: (l, 0, j)),
        ],
        out_specs=pl.BlockSpec((None, r, tn), lambda l, j: (l, 0, j)),
        out_shape=jax.ShapeDtypeStruct((depth, r, n), F32),
        compiler_params=_params(("parallel", "parallel")),
        name="adaln",
    )(cvec, ada_w, ada_b.reshape(depth, 1, n))


def _modulate(u_scr, x_ref, sc_ref, sh_ref):
    @pl.when(pl.program_id(1) == 0)
    def _():
        u_scr[...] = (x_ref[...] * (1.0 + sc_ref[...]) + sh_ref[...]).astype(BF16)


def _sigmoid(x):
    return 0.5 * jnp.tanh(0.5 * x) + 0.5


def _norm_rope_store(acc, ta, tb, o_ref):
    tn = acc.shape[1]
    rr = lax.broadcasted_iota(I32, (HEAD_DIM, HEAD_DIM), 0)
    cc = lax.broadcasted_iota(I32, (HEAD_DIM, HEAD_DIM), 1)
    swap = jnp.where((rr ^ 1) == cc, 1.0, 0.0).astype(BF16)
    mean_w = jnp.full((HEAD_DIM, HEAD_DIM), 1.0 / HEAD_DIM, BF16)
    for h in range(tn // HEAD_DIM):
        cols = slice(h * HEAD_DIM, (h + 1) * HEAD_DIM)
        a = acc[:, cols]
        ms = jnp.dot((a * a).astype(BF16), mean_w, preferred_element_type=F32)
        partner = jnp.dot(a.astype(BF16), swap, preferred_element_type=F32)
        o_ref[:, cols] = (lax.rsqrt(ms + NORM_EPS) * (a * ta + partner * tb)).astype(BF16)


def _proj_qk_kernel(x_ref, sc_ref, sh_ref, w_ref, ta_ref, tb_ref, o_ref, u_scr):
    _modulate(u_scr, x_ref, sc_ref, sh_ref)
    acc = jnp.dot(u_scr[...], w_ref[...], preferred_element_type=F32)
    _norm_rope_store(acc, ta_ref[...], tb_ref[...], o_ref)


def _proj_plain_kernel(x_ref, sc_ref, sh_ref, w_ref, o_ref, u_scr):
    _modulate(u_scr, x_ref, sc_ref, sh_ref)
    o_ref[...] = jnp.dot(u_scr[...], w_ref[...], preferred_element_type=F32).astype(BF16)


def _proj_fused_kernel(x_ref, sc_ref, sh_ref, w_ref, wx_ref, ta_ref, tb_ref, qk_ref, vcb_ref, z_ref, g_ref, u_scr,
                       *, n_qk, n_vcb, n_z):
    _modulate(u_scr, x_ref, sc_ref, sh_ref)
    j = pl.program_id(1)
    z0 = n_qk + n_vcb

    @pl.when(j < n_qk)
    def _():
        acc = jnp.dot(u_scr[...], w_ref[...], preferred_element_type=F32)
        _norm_rope_store(acc, ta_ref[...], tb_ref[...], qk_ref)

    @pl.when((j >= n_qk) & (j < z0))
    def _():
        vcb_ref[...] = jnp.dot(u_scr[...], w_ref[...], preferred_element_type=F32).astype(BF16)

    @pl.when((j >= z0) & (j < z0 + n_z))
    def _():
        u = u_scr[...]
        c = jnp.dot(u, w_ref[...], preferred_element_type=F32)
        xin = jnp.dot(u, wx_ref[...], preferred_element_type=F32)
        z_ref[...] = (c * xin).astype(BF16)

    @pl.when(j >= z0 + n_z)
    def _():
        acc = jnp.dot(u_scr[...], w_ref[...], preferred_element_type=F32)
        g_ref[...] = _sigmoid(acc).astype(BF16)


class _Rows:
    def __init__(self, n_rows, tm, batch, seq, n_lat, tile0=0):
        assert n_rows % tm == 0 and seq % tm == 0 and n_lat % tm == 0
        self.n_rows, self.tm, self.batch, self.tile0 = n_rows, tm, batch, tile0
        self.n_tiles = n_rows // tm
        self.lat_tiles = n_lat // tm
        self.per_seq = seq // tm

    def state_tile(self, i):
        return i + self.tile0

    def mod_row(self, i):
        t = i + self.tile0
        return jnp.where(t < self.lat_tiles, t // self.per_seq, self.batch)

    def rope_blk(self, i):
        t = i + self.tile0
        return jnp.where(t < self.lat_tiles, t % self.per_seq, self.per_seq)


def _proj_common_specs(rows, d):
    return [
        pl.BlockSpec((rows.tm, d), lambda i, j: (rows.state_tile(i), 0)),
        pl.BlockSpec((None, 1, d), lambda i, j: (rows.mod_row(i), 0, 0)),
        pl.BlockSpec((None, 1, d), lambda i, j: (rows.mod_row(i), 0, 0)),
    ]


def _proj(kind, rows, x, sc, sh, w, layer, col_off, n_cols, tn, extra=(), out_map=None, name="proj"):
    d = x.shape[1]
    tm = rows.tm
    assert col_off % tn == 0
    off = col_off // tn
    in_specs = _proj_common_specs(rows, d) + [pl.BlockSpec((None, d, tn), lambda i, j: (layer, 0, off + j))]
    operands = [x, sc, sh, w]
    if kind == "qk":
        tab_a, tab_b, q_tiles = extra
        tab_spec = pl.BlockSpec((None, tm, HEAD_DIM), lambda i, j: (jnp.where(j < q_tiles, 0, 1), rows.rope_blk(i), 0))
        in_specs += [tab_spec, tab_spec]
        operands += [tab_a, tab_b]
    if out_map is None:
        out_map = lambda i, j: (i, j)
    return pl.pallas_call(
        {"qk": _proj_qk_kernel, "plain": _proj_plain_kernel}[kind],
        grid=(rows.n_tiles, n_cols // tn),
        in_specs=in_specs,
        out_specs=pl.BlockSpec((tm, tn), out_map),
        out_shape=jax.ShapeDtypeStruct((rows.n_rows, n_cols), BF16),
        scratch_shapes=[pltpu.VMEM((tm, d), BF16)],
        compiler_params=_params(("parallel", "arbitrary")),
        name=name,
    )(*operands)


def _proj_fused(rows, x, sc, sh, w, layer, tab_a, tab_b, d_q, d_kv, d_conv, tn):
    d = x.shape[1]
    tm = rows.tm
    n_qk, n_vt, n_vcb, n_z, n_g = (d_q + d_kv) // tn, d_kv // tn, (d_kv + d_conv) // tn, d_conv // tn, 2 * d // tn
    q_tiles = d_q // tn
    z0 = n_qk + n_vcb

    def clamp(v, n):
        return jnp.clip(v, 0, n - 1)

    tab_spec = pl.BlockSpec((None, tm, HEAD_DIM), lambda i, j: (jnp.where(j < q_tiles, 0, 1), rows.rope_blk(i), 0))
    in_specs = _proj_common_specs(rows, d) + [
        pl.BlockSpec((None, d, tn), lambda i, j: (layer, 0, j + jnp.where(j >= z0 + n_z, n_z, 0))),
        pl.BlockSpec((None, d, tn), lambda i, j: (layer, 0, z0 + n_z + clamp(j - z0, n_z))),
        tab_spec, tab_spec,
    ]
    out_specs = [
        pl.BlockSpec((tm, tn), lambda i, j: (i, clamp(j, n_qk))),
        pl.BlockSpec((tm, tn), lambda i, j: (i, (clamp(j - n_qk, n_vcb) + n_vcb - n_vt) % n_vcb)),
        pl.BlockSpec((tm, tn), lambda i, j: (i, clamp(j - z0, n_z))),
        pl.BlockSpec((tm, tn), lambda i, j: (i, clamp(j - z0 - n_z, n_g))),
    ]
    out_shape = [jax.ShapeDtypeStruct((rows.n_rows, n * tn), BF16) for n in (n_qk, n_vcb, n_z, n_g)]
    return pl.pallas_call(
        functools.partial(_proj_fused_kernel, n_qk=n_qk, n_vcb=n_vcb, n_z=n_z),
        grid=(rows.n_tiles, n_qk + n_vcb + n_z + n_g),
        in_specs=in_specs,
        out_specs=out_specs,
        out_shape=out_shape,
        scratch_shapes=[pltpu.VMEM((tm, d), BF16)],
        compiler_params=_params(("parallel", "arbitrary")),
        name="proj_fused",
    )(x, sc, sh, w, w, tab_a, tab_b)


def _attn_heads(q_ref, kl_ref, kc_ref, vl_ext, vc_ext, o_ref, with_latent):
    nt = (((1,), (1,)), ((), ()))
    for h in range(GQA_GROUP):
        cols = slice(h * HEAD_DIM, (h + 1) * HEAD_DIM)
        q = q_ref[:, cols]
        s_c = lax.dot_general(q, kc_ref[...], nt, preferred_element_type=F32)
        m = jnp.max(s_c, axis=-1, keepdims=True)
        if with_latent:
            s_l = lax.dot_general(q, kl_ref[...], nt, preferred_element_type=F32)
            m = jnp.maximum(m, jnp.max(s_l, axis=-1, keepdims=True))
        p_c = jnp.exp((s_c - m).astype(BF16))
        o = jnp.dot(p_c, vc_ext[...], preferred_element_type=F32)
        if with_latent:
            p_l = jnp.exp((s_l - m).astype(BF16))
            o = o + jnp.dot(p_l, vl_ext[...], preferred_element_type=F32)
        o_ref[:, cols] = (o[:, :HEAD_DIM] / o[:, HEAD_DIM:]).astype(BF16)


def _attn_kernel(q_ref, kl_ref, vl_ref, kc_ref, vc_ref, o_ref, vl_ext, vc_ext, *, lat_steps, ctx_step):
    qi = pl.program_id(2)

    @pl.when(qi == 0)
    def _():
        vl_ext[:, :HEAD_DIM] = vl_ref[...]
        vl_ext[:, HEAD_DIM:] = jnp.ones_like(vl_ref)
        vc_ext[:, :HEAD_DIM] = vc_ref[...]
        vc_ext[:, HEAD_DIM:] = jnp.ones_like(vc_ref)

    if not ctx_step:
        _attn_heads(q_ref, kl_ref, kc_ref, vl_ext, vc_ext, o_ref, True)
        return

    @pl.when(qi < lat_steps)
    def _():
        _attn_heads(q_ref, kl_ref, kc_ref, vl_ext, vc_ext, o_ref, True)

    @pl.when(qi >= lat_steps)
    def _():
        _attn_heads(q_ref, kl_ref, kc_ref, vl_ext, vc_ext, o_ref, False)


def _attention(qk, vcb, qk_ctx, vcb_ctx, ctx_blk0, batch, seq, ctx_len, d_q, d_conv, ctx_queries):
    n_kv = d_q // HEAD_DIM // GQA_GROUP
    tq = ctx_len
    lat_steps = seq // tq
    gw = GQA_GROUP * HEAD_DIM
    k_col = d_q // HEAD_DIM
    v_col = d_conv // HEAD_DIM
    n_rows = batch * seq + (batch * ctx_len if ctx_queries else 0)

    def q_map(b, g, qi):
        return (jnp.where(qi < lat_steps, b * lat_steps + qi, batch * lat_steps + b), g)

    return pl.pallas_call(
        functools.partial(_attn_kernel, lat_steps=lat_steps, ctx_step=ctx_queries),
        grid=(batch, n_kv, lat_steps + (1 if ctx_queries else 0)),
        in_specs=[
            pl.BlockSpec((tq, gw), q_map),
            pl.BlockSpec((seq, HEAD_DIM), lambda b, g, qi: (b, k_col + g)),
            pl.BlockSpec((seq, HEAD_DIM), lambda b, g, qi: (b, v_col + g)),
            pl.BlockSpec((ctx_len, HEAD_DIM), lambda b, g, qi: (ctx_blk0 + b, k_col + g)),
            pl.BlockSpec((ctx_len, HEAD_DIM), lambda b, g, qi: (ctx_blk0 + b, v_col + g)),
        ],
        out_specs=pl.BlockSpec((tq, gw), q_map),
        out_shape=jax.ShapeDtypeStruct((n_rows, d_q), BF16),
        scratch_shapes=[pltpu.VMEM((seq, 2 * HEAD_DIM), BF16), pltpu.VMEM((ctx_len, 2 * HEAD_DIM), BF16)],
        compiler_params=_params(("parallel", "parallel", "arbitrary")),
        name="attention",
    )(qk, qk, vcb, qk_ctx, vcb_ctx)


def _merge_kernel(attn_ref, cb_ref, z_ref, zprev_ref, znext_ref, cw_ref, wa_ref, wc_ref, ga_ref, gc_ref,
                  o_ref, conv_scr, *, rows, seq, ctx_len, chunk):
    i = pl.program_id(0)

    @pl.when(pl.program_id(1) == 0)
    def _():
        tm, d = conv_scr.shape
        seq_len = jnp.where(i < rows.lat_tiles, seq, ctx_len)
        r = lax.broadcasted_iota(I32, (tm, chunk), 0)
        pos = (r + i * tm) & (seq_len - 1)
        first, last = pos == 0, pos == seq_len - 1
        for c0 in range(0, d, chunk):
            cols = slice(c0, c0 + chunk)
            z = z_ref[:, cols].astype(F32)
            z_prev = zprev_ref[:, cols].astype(F32)
            z_next = znext_ref[:, cols].astype(F32)
            before = jnp.where(r == 0, z_prev[7:8], pltpu.roll(z, 1, 0))
            after = jnp.where(r == tm - 1, z_next[0:1], pltpu.roll(z, tm - 1, 0))
            before = jnp.where(first, 0.0, before)
            after = jnp.where(last, 0.0, after)
            conv = cw_ref[0:1, cols] * before + cw_ref[1:2, cols] * z + cw_ref[2:3, cols] * after
            conv_scr[:, cols] = (cb_ref[:, cols].astype(F32) * conv).astype(BF16)

    a = jnp.dot(attn_ref[...], wa_ref[...], preferred_element_type=F32)
    c = jnp.dot(conv_scr[...], wc_ref[...], preferred_element_type=F32)
    o_ref[...] = (ga_ref[...].astype(F32) * a + gc_ref[...].astype(F32) * c).astype(BF16)


def _merge(rows, attn, vcb, z, gates, conv_w, w_attn_o, w_conv_o, layer, seq, ctx_len):
    d = attn.shape[1]
    tm = rows.tm
    tn = _tile(d, 512)
    n_j = d // tn
    halo = tm // 8
    last_halo = rows.n_rows // 8 - 1
    return pl.pallas_call(
        functools.partial(_merge_kernel, rows=rows, seq=seq, ctx_len=ctx_len, chunk=_tile(d, 512)),
        grid=(rows.n_tiles, n_j),
        in_specs=[
            pl.BlockSpec((tm, d), lambda i, j: (i, 0)),
            pl.BlockSpec((tm, d), lambda i, j: (i, 0)),
            pl.BlockSpec((tm, d), lambda i, j: (i, 0)),
            pl.BlockSpec((8, d), lambda i, j: (jnp.maximum(i * halo - 1, 0), 0)),
            pl.BlockSpec((8, d), lambda i, j: (jnp.minimum((i + 1) * halo, last_halo), 0)),
            pl.BlockSpec((None, 3, d), lambda i, j: (layer, 0, 0)),
            pl.BlockSpec((None, d, tn), lambda i, j: (layer, 0, j)),
            pl.BlockSpec((None, d, tn), lambda i, j: (layer, 0, j)),
            pl.BlockSpec((tm, tn), lambda i, j: (i, j)),
            pl.BlockSpec((tm, tn), lambda i, j: (i, n_j + j)),
        ],
        out_specs=pl.BlockSpec((tm, tn), lambda i, j: (i, j)),
        out_shape=jax.ShapeDtypeStruct((rows.n_rows, d), BF16),
        scratch_shapes=[pltpu.VMEM((tm, d), BF16)],
        compiler_params=_params(("parallel", "arbitrary")),
        name="merge",
    )(attn, vcb, z, z, z, conv_w, w_attn_o, w_conv_o, gates, gates)


def _layer_norm(r, g, b):
    mu = jnp.mean(r, axis=-1, keepdims=True)
    c = r - mu
    var = jnp.mean(c * c, axis=-1, keepdims=True)
    return c * lax.rsqrt(var + NORM_EPS) * g + b


def _pack_bf16_pair(lo, hi):
    lo_bits = lax.bitcast_convert_type(lo.astype(BF16).astype(F32), I32)
    hi_bits = lax.bitcast_convert_type(hi.astype(BF16).astype(F32), I32)
    return lax.shift_right_logical(lo_bits, jnp.full_like(lo_bits, 16)) | (hi_bits & jnp.int32(-65536))


def _unpack_bf16_pair(w):
    lo = lax.bitcast_convert_type(lax.shift_left(w, jnp.full_like(w, 16)), F32)
    hi = lax.bitcast_convert_type(w & jnp.int32(-65536), F32)
    return lo, hi


def _store_token_rows(ref, packed):
    tm, half = packed.shape
    n_chunks = half // LANES
    for c in range(n_chunks):
        ref[pl.ds(c, tm, stride=n_chunks), :] = packed[:, c * LANES:(c + 1) * LANES]


def _load_token_chunk(ref, lead, c, tm, n_chunks):
    return ref[(*lead, pl.ds(c, tm, stride=n_chunks), slice(None))]


def _mix_router_kernel(y_ref, wmix_ref, x_ref, g1_ref, lng_ref, lnb_ref, sc2_ref, sh2_ref, rw2_ref, rwh_ref, rb_ref,
                       x1_ref, vp_ref, meta_ref, cnt_ref, run_scr, *, alpha, n_experts):
    @pl.when(pl.program_id(0) == 0)
    def _():
        run_scr[...] = jnp.zeros_like(run_scr)

    mix = jnp.dot(y_ref[...], wmix_ref[...], preferred_element_type=F32)
    x1 = _layer_norm(alpha * x_ref[...] + g1_ref[...] * mix, lng_ref[...], lnb_ref[...])
    x1_ref[...] = x1
    v = x1 * (1.0 + sc2_ref[...]) + sh2_ref[...]
    tm, d = v.shape
    _store_token_rows(vp_ref, _pack_bf16_pair(v[:, : d // 2], v[:, d // 2:]))

    v_hi = v.astype(BF16)
    v_lo = (v - v_hi.astype(F32)).astype(BF16)
    hh_hl = jnp.dot(v_hi, rw2_ref[...], preferred_element_type=F32)
    lh = jnp.dot(v_lo, rwh_ref[...], preferred_element_type=F32)
    logits = hh_hl[:, :n_experts] + hh_hl[:, n_experts:] + lh + rb_ref[...]
    lane = lax.broadcasted_iota(I32, (tm, n_experts), 1).astype(F32)
    work = logits
    sel, val = [], []
    for _ in range(TOP_K):
        m = jnp.max(work, axis=-1, keepdims=True)
        s = jnp.min(jnp.where(work == m, lane, float(n_experts)), axis=-1, keepdims=True)
        sel.append(s)
        val.append(m)
        work = jnp.where(lane == s, -jnp.inf, work)
    ex = [jnp.exp(v_k - val[0]) for v_k in val]
    den = ex[0] + ex[1] + ex[2] + ex[3]

    onehot = jnp.zeros((tm, n_experts), F32)
    for s in sel:
        onehot = onehot + jnp.where(lane == s, 1.0, 0.0)
    rr = lax.broadcasted_iota(I32, (tm, tm), 0)
    cc = lax.broadcasted_iota(I32, (tm, tm), 1)
    tri = jnp.where(cc <= rr, 1.0, 0.0).astype(BF16)
    incl = jnp.dot(tri, onehot.astype(BF16), preferred_element_type=F32)
    before = run_scr[...] + incl - onehot

    mlane = lax.broadcasted_iota(I32, (tm, LANES), 1)
    meta = jnp.zeros((tm, LANES), F32)
    for k in range(TOP_K):
        pos_k = jnp.sum(jnp.where(lane == sel[k], before, 0.0), axis=-1, keepdims=True)
        meta = jnp.where(mlane == META_IDX + k, sel[k], meta)
        meta = jnp.where(mlane == META_GATE + k, ex[k] / den, meta)
        meta = jnp.where(mlane == META_POS + k, pos_k, meta)
    meta_ref[...] = meta
    run_scr[...] = run_scr[...] + incl[tm - 1:tm, :]
    cnt_ref[...] = run_scr[...]


def _mix_router(rows, y, w_mix, layer, x, g1, ln_g, ln_b, sc2, sh2, router_w, router_b, alpha):
    d = x.shape[1]
    tm = rows.tm
    e = router_w.shape[1]
    n_chunks = d // 2 // LANES
    rw_hi = router_w.astype(BF16)
    rw_lo = (router_w - rw_hi.astype(F32)).astype(BF16)
    mod_spec = pl.BlockSpec((None, 1, d), lambda i: (rows.mod_row(i), 0, 0))
    vec_spec = pl.BlockSpec((1, d), lambda i: (0, 0))
    return pl.pallas_call(
        functools.partial(_mix_router_kernel, alpha=alpha, n_experts=e),
        grid=(rows.n_tiles,),
        in_specs=[
            pl.BlockSpec((tm, d), lambda i: (i, 0)),
            pl.BlockSpec((None, d, d), lambda i: (layer, 0, 0)),
            pl.BlockSpec((tm, d), lambda i: (i, 0)),
            mod_spec, vec_spec, vec_spec, mod_spec, mod_spec,
            pl.BlockSpec((d, 2 * e), lambda i: (0, 0)),
            pl.BlockSpec((d, e), lambda i: (0, 0)),
            pl.BlockSpec((1, e), lambda i: (0, 0)),
        ],
        out_specs=[
            pl.BlockSpec((tm, d), lambda i: (i, 0)),
            pl.BlockSpec((tm * n_chunks, LANES), lambda i: (i, 0)),
            pl.BlockSpec((tm, LANES), lambda i: (i, 0)),
            pl.BlockSpec((1, e), lambda i: (0, 0)),
        ],
        out_shape=[
            jax.ShapeDtypeStruct((rows.n_rows, d), F32),
            jax.ShapeDtypeStruct((rows.n_rows * n_chunks, LANES), I32),
            jax.ShapeDtypeStruct((rows.n_rows, LANES), F32),
            jax.ShapeDtypeStruct((1, e), F32),
        ],
        scratch_shapes=[pltpu.VMEM((1, e), F32)],
        compiler_params=_params(("arbitrary",)),
        name="mix_ln_router",
    )(y, w_mix, x, g1, ln_g.reshape(1, d), ln_b.reshape(1, d), sc2, sh2,
      jnp.concatenate([rw_hi, rw_lo], axis=1), rw_hi, router_b.reshape(1, e))


def _dispatch_kernel(pad_lo_ref, pad_hi_ref, dest_ref, vp_ref, wu_ref, wd_ref, xg_ref, wub_ref, wdb_ref,
                     zero_scr, sem, zero_sem, *, tm, n_chunks, cast_steps):
    def slot_rows(ref, s):
        return ref.at[pl.ds(pl.multiple_of(s * n_chunks, n_chunks), n_chunks), :]

    @pl.when(pl.program_id(0) == 0)
    def _():
        zero_scr[...] = jnp.zeros_like(zero_scr)

        def zero_copy(s):
            return pltpu.make_async_copy(zero_scr, slot_rows(xg_ref, s), zero_sem)

        def start(s, carry):
            zero_copy(s).start()
            return carry

        def wait(s, carry):
            zero_copy(s).wait()
            return carry

        for e in range(pad_lo_ref.shape[0]):
            lax.fori_loop(pad_lo_ref[e], pad_hi_ref[e], start, 0)
        for e in range(pad_lo_ref.shape[0]):
            lax.fori_loop(pad_lo_ref[e], pad_hi_ref[e], wait, 0)

    def row_copy(r, d):
        return pltpu.make_async_copy(slot_rows(vp_ref, r), slot_rows(xg_ref, d), sem)

    def issue(r, carry):
        for k in range(TOP_K):
            row_copy(r, dest_ref[r * TOP_K + k]).start(priority=k % 2)
        return carry

    lax.fori_loop(0, tm, issue, 0, unroll=DMA_UNROLL)

    @pl.when(pl.program_id(0) < cast_steps)
    def _():
        wub_ref[...] = wu_ref[...].astype(BF16)
        wdb_ref[...] = wd_ref[...].astype(BF16)

    def drain(r, carry):
        row_copy(0, 0).wait()
        return carry

    lax.fori_loop(0, tm * TOP_K, drain, 0, unroll=DMA_UNROLL)


def _dispatch(vp, dest_flat, pad_lo, pad_hi, n_slots, tm, n_chunks, w_up, w_down, layer):
    n_rows = vp.shape[0] // n_chunks
    n_steps = n_rows // tm
    depth, e, d, f2 = w_up.shape
    cast_steps = 1 << (n_steps.bit_length() - 1)
    up_rows, dn_rows = e * d // cast_steps, e * (f2 // 2) // cast_steps
    assert up_rows % 16 == 0 and dn_rows % 16 == 0

    def w_in_map(i, lo, hi):
        return (layer * cast_steps + jnp.minimum(i, cast_steps - 1), 0)

    def w_out_map(i, lo, hi):
        return (jnp.minimum(i, cast_steps - 1), 0)

    grid_spec = pltpu.PrefetchScalarGridSpec(
        num_scalar_prefetch=2,
        grid=(n_steps,),
        in_specs=[
            pl.BlockSpec((tm * TOP_K,), lambda i, lo, hi: (i,), memory_space=pltpu.SMEM),
            pl.BlockSpec((tm * n_chunks, LANES), lambda i, lo, hi: (i, 0)),
            pl.BlockSpec((up_rows, f2), w_in_map),
            pl.BlockSpec((dn_rows, d), w_in_map),
        ],
        out_specs=[
            pl.BlockSpec(memory_space=pl.ANY),
            pl.BlockSpec((up_rows, f2), w_out_map),
            pl.BlockSpec((dn_rows, d), w_out_map),
        ],
        scratch_shapes=[pltpu.VMEM((n_chunks, LANES), I32), pltpu.SemaphoreType.DMA(()), pltpu.SemaphoreType.DMA(())],
    )
    xg, w_up_b, w_down_b = pl.pallas_call(
        functools.partial(_dispatch_kernel, tm=tm, n_chunks=n_chunks, cast_steps=cast_steps),
        grid_spec=grid_spec,
        out_shape=[
            jax.ShapeDtypeStruct((n_slots * n_chunks, LANES), I32),
            jax.ShapeDtypeStruct((e * d, f2), BF16),
            jax.ShapeDtypeStruct((e * (f2 // 2), d), BF16),
        ],
        compiler_params=_params(("arbitrary",)),
        name="dispatch",
    )(pad_lo, pad_hi, dest_flat, vp, w_up.reshape(depth * e * d, f2), w_down.reshape(depth * e * (f2 // 2), d))
    return xg, w_up_b.reshape(1, e, d, f2), w_down_b.reshape(1, e, f2 // 2, d)


def _expert_kernel(be_ref, nused_ref, xg_ref, wup_ref, bup_ref, wdn_ref, bdn_ref, y_ref, x_scr):
    del be_ref
    i = pl.program_id(0)

    @pl.when(i < nused_ref[0])
    def _():
        bm, d = x_scr.shape
        half = d // 2
        n_chunks = half // LANES
        for c in range(n_chunks):
            lo, hi = _unpack_bf16_pair(_load_token_chunk(xg_ref, (), c, bm, n_chunks))
            x_scr[:, c * LANES:(c + 1) * LANES] = lo.astype(BF16)
            x_scr[:, half + c * LANES:half + (c + 1) * LANES] = hi.astype(BF16)
        h = jnp.dot(x_scr[...], wup_ref[...], preferred_element_type=F32) + bup_ref[...]
        f = h.shape[1] // 2
        glu = jnp.minimum(h[:, :f], SWIGLU_LIMIT)
        lin = jnp.clip(h[:, f:], -SWIGLU_LIMIT, SWIGLU_LIMIT)
        act = glu * _sigmoid(SWIGLU_ALPHA * glu) * (lin + 1.0)
        y = jnp.dot(act.astype(BF16), wdn_ref[...], preferred_element_type=F32) + bdn_ref[...]
        _store_token_rows(y_ref, _pack_bf16_pair(y[:, :half], y[:, half:]))

    @pl.when(i >= nused_ref[0])
    def _():
        y_ref[...] = jnp.zeros_like(y_ref)


def _experts(xg, block_e, n_used, w_up, b_up, w_down, b_down, layer):
    _, e, d, f2 = w_up.shape
    n_chunks = d // 2 // LANES
    bm = EXPERT_ROWS
    n_blocks = xg.shape[0] // n_chunks // bm

    def x_map(i, be, nu):
        return (jnp.maximum(jnp.minimum(i, nu[0] - 1), 0), 0)

    def w_map(i, be, nu):
        return (0, be[i], 0, 0)

    def b_map(i, be, nu):
        return (layer, be[i], 0, 0)

    grid_spec = pltpu.PrefetchScalarGridSpec(
        num_scalar_prefetch=2,
        grid=(n_blocks,),
        in_specs=[
            pl.BlockSpec((bm * n_chunks, LANES), x_map),
            pl.BlockSpec((None, None, d, f2), w_map),
            pl.BlockSpec((None, None, 1, f2), b_map),
            pl.BlockSpec((None, None, f2 // 2, d), w_map),
            pl.BlockSpec((None, None, 1, d), b_map),
        ],
        out_specs=pl.BlockSpec((bm * n_chunks, LANES), lambda i, be, nu: (i, 0)),
        scratch_shapes=[pltpu.VMEM((bm, d), BF16)],
    )
    depth = b_up.shape[0]
    return pl.pallas_call(
        _expert_kernel,
        grid_spec=grid_spec,
        out_shape=jax.ShapeDtypeStruct(xg.shape, I32),
        compiler_params=_params(("arbitrary",)),
        name="experts",
    )(block_e, n_used, xg, w_up, b_up.reshape(depth, e, 1, f2), w_down, b_down.reshape(depth, e, 1, d))


def _combine_kernel(dcur_ref, dnext_ref, meta_ref, x1_ref, g2_ref, lng_ref, lnb_ref, y_hbm,
                    o_ref, buf, f_scr, sem, *, alpha):
    i = pl.program_id(0)
    n = pl.num_programs(0)
    tm, d = x1_ref.shape
    half = d // 2
    n_chunks = half // LANES

    def row_copy(d_slot, slot, r, k):
        src = y_hbm.at[pl.ds(pl.multiple_of(d_slot * n_chunks, n_chunks), n_chunks), :]
        dst = buf.at[slot, k, pl.ds(pl.multiple_of(r * n_chunks, n_chunks), n_chunks), :]
        return pltpu.make_async_copy(src, dst, sem.at[slot])

    def issue_tile(dref, slot):
        def body(r, carry):
            for k in range(TOP_K):
                row_copy(dref[r * TOP_K + k], slot, r, k).start(priority=k % 2)
            return carry
        lax.fori_loop(0, tm, body, 0, unroll=DMA_UNROLL)

    @pl.when(i == 0)
    def _():
        issue_tile(dcur_ref, 0)

    @pl.when(i + 1 < n)
    def _():
        issue_tile(dnext_ref, (i + 1) % 2)

    slot = i % 2

    def drain(r, carry):
        row_copy(0, slot, 0, 0).wait()
        return carry

    lax.fori_loop(0, tm * TOP_K, drain, 0, unroll=DMA_UNROLL)

    meta = meta_ref[...]
    gate = [meta[:, META_GATE + k:META_GATE + k + 1] for k in range(TOP_K)]
    for c in range(n_chunks):
        f_lo = jnp.zeros((tm, LANES), F32)
        f_hi = jnp.zeros((tm, LANES), F32)
        for k in range(TOP_K):
            lo, hi = _unpack_bf16_pair(_load_token_chunk(buf, (slot, k), c, tm, n_chunks))
            f_lo = f_lo + gate[k] * lo
            f_hi = f_hi + gate[k] * hi
        f_scr[:, c * LANES:(c + 1) * LANES] = f_lo
        f_scr[:, half + c * LANES:half + (c + 1) * LANES] = f_hi
    o_ref[...] = _layer_norm(alpha * x1_ref[...] + g2_ref[...] * f_scr[...], lng_ref[...], lnb_ref[...])


def _combine(rows, dest_flat, meta, x1, g2, ln_g, ln_b, y_slots, alpha):
    d = x1.shape[1]
    tm = rows.tm
    n_chunks = d // 2 // LANES
    last = rows.n_tiles - 1
    vec_spec = pl.BlockSpec((1, d), lambda i: (0, 0))
    return pl.pallas_call(
        functools.partial(_combine_kernel, alpha=alpha),
        grid=(rows.n_tiles,),
        in_specs=[
            pl.BlockSpec((tm * TOP_K,), lambda i: (i,), memory_space=pltpu.SMEM),
            pl.BlockSpec((tm * TOP_K,), lambda i: (jnp.minimum(i + 1, last),), memory_space=pltpu.SMEM),
            pl.BlockSpec((tm, LANES), lambda i: (i, 0)),
            pl.BlockSpec((tm, d), lambda i: (i, 0)),
            pl.BlockSpec((None, 1, d), lambda i: (rows.mod_row(i), 0, 0)),
            vec_spec, vec_spec,
            pl.BlockSpec(memory_space=pl.ANY),
        ],
        out_specs=pl.BlockSpec((tm, d), lambda i: (i, 0)),
        out_shape=jax.ShapeDtypeStruct((rows.n_rows, d), F32),
        scratch_shapes=[pltpu.VMEM((2, TOP_K, tm * n_chunks, LANES), I32), pltpu.VMEM((tm, d), F32),
                        pltpu.SemaphoreType.DMA((2,))],
        compiler_params=_params(("arbitrary",)),
        name="combine_ln",
    )(dest_flat, dest_flat, meta, x1, g2, ln_g.reshape(1, d), ln_b.reshape(1, d), y_slots)


def _slot_layout(meta, counts, n_rows):
    e = counts.shape[1]
    bm = EXPERT_ROWS
    cnt = counts[0].astype(I32)
    padded = (cnt + bm - 1) // bm * bm
    pend = jnp.cumsum(padded)
    pstart = pend - padded
    idx = meta[:, META_IDX:META_IDX + TOP_K].astype(I32)
    pos = meta[:, META_POS:META_POS + TOP_K].astype(I32)
    onehot = idx[..., None] == jnp.arange(e, dtype=I32)
    dest = jnp.sum(jnp.where(onehot, pstart, 0), axis=-1) + pos
    n_blocks = -(-(n_rows * TOP_K + e * (bm - 1)) // bm)
    n_used = pend[-1] // bm
    blk = jnp.minimum(jnp.arange(n_blocks, dtype=I32), n_used - 1) * bm
    block_e = jnp.minimum(jnp.sum(pend[None, :] <= blk[:, None], axis=-1), e - 1).astype(I32)
    pad_lo = jnp.concatenate([pstart + cnt, pend[-1:]]).astype(I32)
    pad_hi = jnp.concatenate([pend, jnp.full((1,), n_blocks * bm, I32)]).astype(I32)
    return dest.reshape(-1), block_e, n_used.reshape(1).astype(I32), pad_lo, pad_hi, n_blocks * bm


def _rope_tables(seq, pad_rows):
    rows = seq // GRID_W
    row = jnp.repeat(jnp.arange(rows, dtype=I32), GRID_W).astype(F32)
    col = jnp.tile(jnp.arange(GRID_W, dtype=I32), rows).astype(F32)
    half = HEAD_DIM // 2
    inv = ROPE_THETA ** (-jnp.arange(0, half, 2, dtype=F32) / half)
    ang = jnp.concatenate([row[:, None] * inv, col[:, None] * inv], axis=-1)
    cos = jnp.repeat(jnp.cos(ang), 2, axis=-1)
    sin = jnp.repeat(jnp.sin(ang), 2, axis=-1) * jnp.tile(jnp.array([-1.0, 1.0], F32), half)
    cos = jnp.concatenate([cos, jnp.ones((pad_rows, HEAD_DIM), F32)], axis=0)
    sin = jnp.concatenate([sin, jnp.zeros((pad_rows, HEAD_DIM), F32)], axis=0)
    return cos, sin


def kernel(x, c, ctx, c_ctx, ada_w, ada_b, w_in, q_norm_g, k_norm_g, conv_w, w_attn_o, w_conv_o, w_mix_o,
           ln1_g, ln1_b, router_w, router_b, w_up, b_up, w_down, b_down, ln2_g, ln2_b):
    batch, seq, d = x.shape
    ctx_len = ctx.shape[1]
    depth = ada_w.shape[0]
    n_q = d // HEAD_DIM
    d_q, d_kv, d_conv = d, n_q // GQA_GROUP * HEAD_DIM, d
    o_k, o_v = d_q, d_q + d_kv
    o_cb = o_v + d_kv
    o_cc, o_cx = o_cb + d_conv, o_cb + 2 * d_conv
    o_ga = o_cx + d_conv
    n_lat, n_ctx = batch * seq, batch * ctx_len
    n_tok = n_lat + n_ctx
    alpha = (2 * depth) ** 0.25
    assert seq & (seq - 1) == 0 and ctx_len & (ctx_len - 1) == 0 and seq % ctx_len == 0 and seq % GRID_W == 0

    tn = next(t for t in (512, 256, 128) if all(o % t == 0 for o in (o_k, o_v, o_cb, o_cc, o_cx, o_ga)))
    row_unit = math.gcd(seq, n_ctx)
    tm_proj, tm_merge, tm_small = _tile(row_unit, 1024), _tile(row_unit, 1024), _tile(row_unit, 256)

    mod_rows = -(-(batch + 1) // 8) * 8
    cvec = jnp.zeros((mod_rows, d), F32).at[:batch].set(c).at[batch].set(c_ctx)
    mod = _adaln(cvec, ada_w, ada_b).reshape(depth, mod_rows, 6, 1, d)

    cos, sin = _rope_tables(seq, tm_proj)
    scale = HEAD_DIM ** -0.5
    state = jnp.concatenate([x.reshape(n_lat, d), ctx.reshape(n_ctx, d)], axis=0)
    n_chunks = d // 2 // LANES
    w_in_b, w_attn_b, w_conv_b, w_mix_b = (w.astype(BF16) for w in (w_in, w_attn_o, w_conv_o, w_mix_o))
    pair = jnp.arange(HEAD_DIM) ^ 1
    n_vt, n_vcb = d_kv // tn, (d_kv + d_conv) // tn

    for l in range(depth):
        last = l == depth - 1
        sh1, sc1, g1, sh2, sc2, g2 = [mod[l, :, m] for m in range(6)]
        gq, gk = q_norm_g[l] * scale, k_norm_g[l]
        tab_a = jnp.stack([cos * gq, cos * gk])
        tab_b = jnp.stack([sin * gq[pair], sin * gk[pair]])
        n_main = n_lat if last else n_tok

        rows_main = _Rows(n_main, tm_proj, batch, seq, n_lat)
        qk, vcb, z, gates = _proj_fused(rows_main, state, sc1, sh1, w_in_b, l, tab_a, tab_b, d_q, d_kv, d_conv, tn)
        if last:
            rows_ctx = _Rows(n_ctx, tm_proj, batch, seq, n_lat, tile0=n_lat // tm_proj)
            qk_ctx = _proj("qk", rows_ctx, state, sc1, sh1, w_in_b, l, 0, d_q + d_kv, tn,
                           extra=(tab_a, tab_b, d_q // tn), name="proj_qk_ctx")
            vcb_ctx = _proj("plain", rows_ctx, state, sc1, sh1, w_in_b, l, o_v, d_kv + d_conv, tn,
                            out_map=lambda i, j: (i, (j + n_vcb - n_vt) % n_vcb), name="proj_vcb_ctx")
            ctx_blk0 = 0
        else:
            qk_ctx, vcb_ctx, ctx_blk0 = qk, vcb, n_lat // ctx_len

        attn = _attention(qk, vcb, qk_ctx, vcb_ctx, ctx_blk0, batch, seq, ctx_len, d_q, d_conv, ctx_queries=not last)

        rows_m = _Rows(n_main, tm_merge, batch, seq, n_lat)
        y = _merge(rows_m, attn, vcb, z, gates, conv_w, w_attn_b, w_conv_b, l, seq, ctx_len)

        rows_s = _Rows(n_main, tm_small, batch, seq, n_lat)
        x1, vp, meta, counts = _mix_router(rows_s, y, w_mix_b, l, state, g1, ln1_g[l], ln1_b[l],
                                           sc2, sh2, router_w[l], router_b[l], alpha)
        dest, block_e, n_used, pad_lo, pad_hi, n_slots = _slot_layout(meta, counts, n_main)
        xg, w_up_b, w_down_b = _dispatch(vp, dest, pad_lo, pad_hi, n_slots, tm_small, n_chunks, w_up, w_down, l)
        y_slots = _experts(xg, block_e, n_used, w_up_b, b_up, w_down_b, b_down, l)
        state = _combine(rows_s, dest, meta, x1, g2, ln2_g[l], ln2_b[l], y_slots, alpha)

    return state[:n_lat].reshape(batch, seq, d)
```

```python
import functools
import math

import jax
import jax.numpy as jnp
from jax import lax
from jax.experimental import pallas as pl
from jax.experimental.pallas import tpu as pltpu

F32 = jnp.float32
BF16 = jnp.bfloat16
I32 = jnp.int32

LANES = 128
HEAD_DIM = 128
GQA_GROUP = 4
GRID_W = 64
TOP_K = 4
SWIGLU_ALPHA = 1.702
SWIGLU_LIMIT = 7.0
ROPE_THETA = 10000.0
NORM_EPS = 1e-6
VMEM_LIMIT = 56 * 1024 * 1024
EXPERT_ROWS = 256
META_IDX, META_GATE, META_POS = 0, 4, 8
DMA_UNROLL = 8


def _params(sem, vmem=VMEM_LIMIT):
    return pltpu.CompilerParams(dimension_semantics=sem, vmem_limit_bytes=vmem)


def _tile(n, pref):
    if n <= pref:
        return n
    t = pref - pref % 8
    while n % t:
        t -= 8
    return t


def _adaln_kernel(c_ref, w_ref, b_ref, o_ref):
    c = c_ref[...]
    s = (c * jax.nn.sigmoid(c)).astype(BF16)
    o_ref[...] = jnp.dot(s, w_ref[...].astype(BF16), preferred_element_type=F32) + b_ref[...]


def _adaln(cvec, ada_w, ada_b):
    depth, d, n = ada_w.shape
    r = cvec.shape[0]
    tn = _tile(n, 1024)
    return pl.pallas_call(
        _adaln_kernel,
        grid=(depth, n // tn),
        in_specs=[
            pl.BlockSpec((r, d), lambda l, j: (0, 0)),
            pl.BlockSpec((None, d, tn), lambda l, j: (l, 0, j)),
            pl.BlockSpec((None, 1, tn), lambda l, j: (l, 0, j)),
        ],
        out_specs=pl.BlockSpec((None, r, tn), lambda l, j: (l, 0, j)),
        out_shape=jax.ShapeDtypeStruct((depth, r, n), F32),
        compiler_params=_params(("parallel", "parallel")),
        name="adaln",
    )(cvec, ada_w, ada_b.reshape(depth, 1, n))


def _modulate(u_scr, x_ref, sc_ref, sh_ref):
    @pl.when(pl.program_id(1) == 0)
    def _():
        u_scr[...] = (x_ref[...] * (1.0 + sc_ref[...]) + sh_ref[...]).astype(BF16)


def _sigmoid(x):
    return 0.5 * jnp.tanh(0.5 * x) + 0.5


def _norm_rope_store(acc, ta, tb, o_ref):
    tn = acc.shape[1]
    rr = lax.broadcasted_iota(I32, (HEAD_DIM, HEAD_DIM), 0)
    cc = lax.broadcasted_iota(I32, (HEAD_DIM, HEAD_DIM), 1)
    swap = jnp.where((rr ^ 1) == cc, 1.0, 0.0).astype(BF16)
    mean_w = jnp.full((HEAD_DIM, HEAD_DIM), 1.0 / HEAD_DIM, BF16)
    for h in range(tn // HEAD_DIM):
        cols = slice(h * HEAD_DIM, (h + 1) * HEAD_DIM)
        a = acc[:, cols]
        ms = jnp.dot((a * a).astype(BF16), mean_w, preferred_element_type=F32)
        partner = jnp.dot(a.astype(BF16), swap, preferred_element_type=F32)
        o_ref[:, cols] = (lax.rsqrt(ms + NORM_EPS) * (a * ta + partner * tb)).astype(BF16)


def _proj_qk_kernel(x_ref, sc_ref, sh_ref, w_ref, ta_ref, tb_ref, o_ref, u_scr):
    _modulate(u_scr, x_ref, sc_ref, sh_ref)
    acc = jnp.dot(u_scr[...], w_ref[...], preferred_element_type=F32)
    _norm_rope_store(acc, ta_ref[...], tb_ref[...], o_ref)


def _proj_plain_kernel(x_ref, sc_ref, sh_ref, w_ref, o_ref, u_scr):
    _modulate(u_scr, x_ref, sc_ref, sh_ref)
    o_ref[...] = jnp.dot(u_scr[...], w_ref[...], preferred_element_type=F32).astype(BF16)


def _proj_fused_kernel(x_ref, sc_ref, sh_ref, w_ref, wx_ref, ta_ref, tb_ref, wu_ref, wd_ref,
                       qk_ref, vcb_ref, z_ref, g_ref, wub_ref, wdb_ref, u_scr, *, n_qk, n_vcb, n_z, cast_steps):
    _modulate(u_scr, x_ref, sc_ref, sh_ref)
    j = pl.program_id(1)
    z0 = n_qk + n_vcb

    @pl.when(pl.program_id(0) * pl.num_programs(1) + j < cast_steps)
    def _():
        wub_ref[...] = wu_ref[...].astype(BF16)
        wdb_ref[...] = wd_ref[...].astype(BF16)

    @pl.when(j < n_qk)
    def _():
        acc = jnp.dot(u_scr[...], w_ref[...], preferred_element_type=F32)
        _norm_rope_store(acc, ta_ref[...], tb_ref[...], qk_ref)

    @pl.when((j >= n_qk) & (j < z0))
    def _():
        vcb_ref[...] = jnp.dot(u_scr[...], w_ref[...], preferred_element_type=F32).astype(BF16)

    @pl.when((j >= z0) & (j < z0 + n_z))
    def _():
        u = u_scr[...]
        c = jnp.dot(u, w_ref[...], preferred_element_type=F32)
        xin = jnp.dot(u, wx_ref[...], preferred_element_type=F32)
        z_ref[...] = (c * xin).astype(BF16)

    @pl.when(j >= z0 + n_z)
    def _():
        acc = jnp.dot(u_scr[...], w_ref[...], preferred_element_type=F32)
        g_ref[...] = _sigmoid(acc).astype(BF16)


class _Rows:
    def __init__(self, n_rows, tm, batch, seq, n_lat, tile0=0):
        assert n_rows % tm == 0 and seq % tm == 0 and n_lat % tm == 0
        self.n_rows, self.tm, self.batch, self.tile0 = n_rows, tm, batch, tile0
        self.n_tiles = n_rows // tm
        self.lat_tiles = n_lat // tm
        self.per_seq = seq // tm

    def state_tile(self, i):
        return i + self.tile0

    def mod_row(self, i):
        t = i + self.tile0
        return jnp.where(t < self.lat_tiles, t // self.per_seq, self.batch)

    def rope_blk(self, i):
        t = i + self.tile0
        return jnp.where(t < self.lat_tiles, t % self.per_seq, self.per_seq)


def _proj_common_specs(rows, d):
    return [
        pl.BlockSpec((rows.tm, d), lambda i, j: (rows.state_tile(i), 0)),
        pl.BlockSpec((None, 1, d), lambda i, j: (rows.mod_row(i), 0, 0)),
        pl.BlockSpec((None, 1, d), lambda i, j: (rows.mod_row(i), 0, 0)),
    ]


def _proj(kind, rows, x, sc, sh, w, layer, col_off, n_cols, tn, extra=(), out_map=None, name="proj"):
    d = x.shape[1]
    tm = rows.tm
    assert col_off % tn == 0
    off = col_off // tn
    in_specs = _proj_common_specs(rows, d) + [pl.BlockSpec((None, d, tn), lambda i, j: (layer, 0, off + j))]
    operands = [x, sc, sh, w]
    if kind == "qk":
        tab_a, tab_b, q_tiles = extra
        tab_spec = pl.BlockSpec((None, tm, HEAD_DIM), lambda i, j: (jnp.where(j < q_tiles, 0, 1), rows.rope_blk(i), 0))
        in_specs += [tab_spec, tab_spec]
        operands += [tab_a, tab_b]
    if out_map is None:
        out_map = lambda i, j: (i, j)
    return pl.pallas_call(
        {"qk": _proj_qk_kernel, "plain": _proj_plain_kernel}[kind],
        grid=(rows.n_tiles, n_cols // tn),
        in_specs=in_specs,
        out_specs=pl.BlockSpec((tm, tn), out_map),
        out_shape=jax.ShapeDtypeStruct((rows.n_rows, n_cols), BF16),
        scratch_shapes=[pltpu.VMEM((tm, d), BF16)],
        compiler_params=_params(("parallel", "arbitrary")),
        name=name,
    )(*operands)


def _proj_fused(rows, x, sc, sh, w, layer, tab_a, tab_b, d_q, d_kv, d_conv, tn, w_up, w_down):
    d = x.shape[1]
    tm = rows.tm
    n_qk, n_vt, n_vcb, n_z, n_g = (d_q + d_kv) // tn, d_kv // tn, (d_kv + d_conv) // tn, d_conv // tn, 2 * d // tn
    q_tiles = d_q // tn
    z0 = n_qk + n_vcb
    n_j = n_qk + n_vcb + n_z + n_g
    depth, e, _, f2 = w_up.shape
    cast_steps = 1 << ((rows.n_tiles * n_j).bit_length() - 1)
    up_rows, dn_rows = e * d // cast_steps, e * (f2 // 2) // cast_steps
    assert up_rows % 16 == 0 and dn_rows % 16 == 0

    def clamp(v, n):
        return jnp.clip(v, 0, n - 1)

    def cast_blk(i, j):
        return jnp.minimum(i * n_j + j, cast_steps - 1)

    tab_spec = pl.BlockSpec((None, tm, HEAD_DIM), lambda i, j: (jnp.where(j < q_tiles, 0, 1), rows.rope_blk(i), 0))
    in_specs = _proj_common_specs(rows, d) + [
        pl.BlockSpec((None, d, tn), lambda i, j: (layer, 0, j + jnp.where(j >= z0 + n_z, n_z, 0))),
        pl.BlockSpec((None, d, tn), lambda i, j: (layer, 0, z0 + n_z + clamp(j - z0, n_z))),
        tab_spec, tab_spec,
        pl.BlockSpec((up_rows, f2), lambda i, j: (layer * cast_steps + cast_blk(i, j), 0)),
        pl.BlockSpec((dn_rows, d), lambda i, j: (layer * cast_steps + cast_blk(i, j), 0)),
    ]
    out_specs = [
        pl.BlockSpec((tm, tn), lambda i, j: (i, clamp(j, n_qk))),
        pl.BlockSpec((tm, tn), lambda i, j: (i, (clamp(j - n_qk, n_vcb) + n_vcb - n_vt) % n_vcb)),
        pl.BlockSpec((tm, tn), lambda i, j: (i, clamp(j - z0, n_z))),
        pl.BlockSpec((tm, tn), lambda i, j: (i, clamp(j - z0 - n_z, n_g))),
        pl.BlockSpec((up_rows, f2), lambda i, j: (cast_blk(i, j), 0)),
        pl.BlockSpec((dn_rows, d), lambda i, j: (cast_blk(i, j), 0)),
    ]
    out_shape = [jax.ShapeDtypeStruct((rows.n_rows, n * tn), BF16) for n in (n_qk, n_vcb, n_z, n_g)]
    out_shape += [jax.ShapeDtypeStruct((e * d, f2), BF16), jax.ShapeDtypeStruct((e * (f2 // 2), d), BF16)]
    qk, vcb, z, gates, w_up_b, w_down_b = pl.pallas_call(
        functools.partial(_proj_fused_kernel, n_qk=n_qk, n_vcb=n_vcb, n_z=n_z, cast_steps=cast_steps),
        grid=(rows.n_tiles, n_j),
        in_specs=in_specs,
        out_specs=out_specs,
        out_shape=out_shape,
        scratch_shapes=[pltpu.VMEM((tm, d), BF16)],
        compiler_params=_params(("arbitrary", "arbitrary")),
        name="proj_fused",
    )(x, sc, sh, w, w, tab_a, tab_b, w_up.reshape(depth * e * d, f2), w_down.reshape(depth * e * (f2 // 2), d))
    return qk, vcb, z, gates, w_up_b.reshape(1, e, d, f2), w_down_b.reshape(1, e, f2 // 2, d)


def _attn_heads(q_ref, kl_ref, kc_ref, vl_ext, vc_ext, o_ref, with_latent):
    nt = (((1,), (1,)), ((), ()))
    for h in range(GQA_GROUP):
        cols = slice(h * HEAD_DIM, (h + 1) * HEAD_DIM)
        q = q_ref[:, cols]
        s_c = lax.dot_general(q, kc_ref[...], nt, preferred_element_type=F32)
        m = jnp.max(s_c, axis=-1, keepdims=True)
        if with_latent:
            s_l = lax.dot_general(q, kl_ref[...], nt, preferred_element_type=F32)
            m = jnp.maximum(m, jnp.max(s_l, axis=-1, keepdims=True))
        p_c = jnp.exp((s_c - m).astype(BF16))
        o = jnp.dot(p_c, vc_ext[...], preferred_element_type=F32)
        if with_latent:
            p_l = jnp.exp((s_l - m).astype(BF16))
            o = o + jnp.dot(p_l, vl_ext[...], preferred_element_type=F32)
        o_ref[:, cols] = (o[:, :HEAD_DIM] / o[:, HEAD_DIM:]).astype(BF16)


def _attn_kernel(q_ref, kl_ref, vl_ref, kc_ref, vc_ref, o_ref, vl_ext, vc_ext, *, lat_steps, ctx_step):
    qi = pl.program_id(2)

    @pl.when(qi == 0)
    def _():
        vl_ext[:, :HEAD_DIM] = vl_ref[...]
        vl_ext[:, HEAD_DIM:] = jnp.ones_like(vl_ref)
        vc_ext[:, :HEAD_DIM] = vc_ref[...]
        vc_ext[:, HEAD_DIM:] = jnp.ones_like(vc_ref)

    if not ctx_step:
        _attn_heads(q_ref, kl_ref, kc_ref, vl_ext, vc_ext, o_ref, True)
        return

    @pl.when(qi < lat_steps)
    def _():
        _attn_heads(q_ref, kl_ref, kc_ref, vl_ext, vc_ext, o_ref, True)

    @pl.when(qi >= lat_steps)
    def _():
        _attn_heads(q_ref, kl_ref, kc_ref, vl_ext, vc_ext, o_ref, False)


def _attention(qk, vcb, qk_ctx, vcb_ctx, ctx_blk0, batch, seq, ctx_len, d_q, d_conv, ctx_queries):
    n_kv = d_q // HEAD_DIM // GQA_GROUP
    tq = ctx_len
    lat_steps = seq // tq
    gw = GQA_GROUP * HEAD_DIM
    k_col = d_q // HEAD_DIM
    v_col = d_conv // HEAD_DIM
    n_rows = batch * seq + (batch * ctx_len if ctx_queries else 0)

    def q_map(b, g, qi):
        return (jnp.where(qi < lat_steps, b * lat_steps + qi, batch * lat_steps + b), g)

    return pl.pallas_call(
        functools.partial(_attn_kernel, lat_steps=lat_steps, ctx_step=ctx_queries),
        grid=(batch, n_kv, lat_steps + (1 if ctx_queries else 0)),
        in_specs=[
            pl.BlockSpec((tq, gw), q_map),
            pl.BlockSpec((seq, HEAD_DIM), lambda b, g, qi: (b, k_col + g)),
            pl.BlockSpec((seq, HEAD_DIM), lambda b, g, qi: (b, v_col + g)),
            pl.BlockSpec((ctx_len, HEAD_DIM), lambda b, g, qi: (ctx_blk0 + b, k_col + g)),
            pl.BlockSpec((ctx_len, HEAD_DIM), lambda b, g, qi: (ctx_blk0 + b, v_col + g)),
        ],
        out_specs=pl.BlockSpec((tq, gw), q_map),
        out_shape=jax.ShapeDtypeStruct((n_rows, d_q), BF16),
        scratch_shapes=[pltpu.VMEM((seq, 2 * HEAD_DIM), BF16), pltpu.VMEM((ctx_len, 2 * HEAD_DIM), BF16)],
        compiler_params=_params(("parallel", "parallel", "arbitrary")),
        name="attention",
    )(qk, qk, vcb, qk_ctx, vcb_ctx)


def _merge_kernel(attn_ref, cb_ref, z_ref, zprev_ref, znext_ref, cw_ref, wa_ref, wc_ref, ga_ref, gc_ref,
                  o_ref, conv_scr, *, rows, seq, ctx_len, chunk):
    i = pl.program_id(0)

    @pl.when(pl.program_id(1) == 0)
    def _():
        tm, d = conv_scr.shape
        seq_len = jnp.where(i < rows.lat_tiles, seq, ctx_len)
        r = lax.broadcasted_iota(I32, (tm, chunk), 0)
        pos = (r + i * tm) & (seq_len - 1)
        first, last = pos == 0, pos == seq_len - 1
        for c0 in range(0, d, chunk):
            cols = slice(c0, c0 + chunk)
            z = z_ref[:, cols].astype(F32)
            z_prev = zprev_ref[:, cols].astype(F32)
            z_next = znext_ref[:, cols].astype(F32)
            before = jnp.where(r == 0, z_prev[7:8], pltpu.roll(z, 1, 0))
            after = jnp.where(r == tm - 1, z_next[0:1], pltpu.roll(z, tm - 1, 0))
            before = jnp.where(first, 0.0, before)
            after = jnp.where(last, 0.0, after)
            conv = cw_ref[0:1, cols] * before + cw_ref[1:2, cols] * z + cw_ref[2:3, cols] * after
            conv_scr[:, cols] = (cb_ref[:, cols].astype(F32) * conv).astype(BF16)

    a = jnp.dot(attn_ref[...], wa_ref[...], preferred_element_type=F32)
    c = jnp.dot(conv_scr[...], wc_ref[...], preferred_element_type=F32)
    o_ref[...] = (ga_ref[...].astype(F32) * a + gc_ref[...].astype(F32) * c).astype(BF16)


def _merge(rows, attn, vcb, z, gates, conv_w, w_attn_o, w_conv_o, layer, seq, ctx_len):
    d = attn.shape[1]
    tm = rows.tm
    tn = _tile(d, 512)
    n_j = d // tn
    halo = tm // 8
    last_halo = rows.n_rows // 8 - 1
    return pl.pallas_call(
        functools.partial(_merge_kernel, rows=rows, seq=seq, ctx_len=ctx_len, chunk=_tile(d, 512)),
        grid=(rows.n_tiles, n_j),
        in_specs=[
            pl.BlockSpec((tm, d), lambda i, j: (i, 0)),
            pl.BlockSpec((tm, d), lambda i, j: (i, 0)),
            pl.BlockSpec((tm, d), lambda i, j: (i, 0)),
            pl.BlockSpec((8, d), lambda i, j: (jnp.maximum(i * halo - 1, 0), 0)),
            pl.BlockSpec((8, d), lambda i, j: (jnp.minimum((i + 1) * halo, last_halo), 0)),
            pl.BlockSpec((None, 3, d), lambda i, j: (layer, 0, 0)),
            pl.BlockSpec((None, d, tn), lambda i, j: (layer, 0, j)),
            pl.BlockSpec((None, d, tn), lambda i, j: (layer, 0, j)),
            pl.BlockSpec((tm, tn), lambda i, j: (i, j)),
            pl.BlockSpec((tm, tn), lambda i, j: (i, n_j + j)),
        ],
        out_specs=pl.BlockSpec((tm, tn), lambda i, j: (i, j)),
        out_shape=jax.ShapeDtypeStruct((rows.n_rows, d), BF16),
        scratch_shapes=[pltpu.VMEM((tm, d), BF16)],
        compiler_params=_params(("parallel", "arbitrary")),
        name="merge",
    )(attn, vcb, z, z, z, conv_w, w_attn_o, w_conv_o, gates, gates)


def _layer_norm(r, g, b):
    mu = jnp.mean(r, axis=-1, keepdims=True)
    c = r - mu
    var = jnp.mean(c * c, axis=-1, keepdims=True)
    return c * lax.rsqrt(var + NORM_EPS) * g + b


def _pack_bf16_pair(lo, hi):
    lo_bits = lax.bitcast_convert_type(lo.astype(BF16).astype(F32), I32)
    hi_bits = lax.bitcast_convert_type(hi.astype(BF16).astype(F32), I32)
    return lax.shift_right_logical(lo_bits, jnp.full_like(lo_bits, 16)) | (hi_bits & jnp.int32(-65536))


def _unpack_bf16_pair(w):
    lo = lax.bitcast_convert_type(lax.shift_left(w, jnp.full_like(w, 16)), F32)
    hi = lax.bitcast_convert_type(w & jnp.int32(-65536), F32)
    return lo, hi


def _store_token_rows(ref, packed):
    tm, half = packed.shape
    n_chunks = half // LANES
    for c in range(n_chunks):
        ref[pl.ds(c, tm, stride=n_chunks), :] = packed[:, c * LANES:(c + 1) * LANES]


def _load_token_chunk(ref, lead, c, tm, n_chunks):
    return ref[(*lead, pl.ds(c, tm, stride=n_chunks), slice(None))]


def _mix_router_kernel(y_ref, wmix_ref, x_ref, g1_ref, lng_ref, lnb_ref, sc2_ref, sh2_ref, rw2_ref, rwh_ref, rb_ref,
                       x1_ref, vp_ref, meta_ref, cnt_ref, run_scr, *, alpha, n_experts):
    @pl.when(pl.program_id(0) == 0)
    def _():
        run_scr[...] = jnp.zeros_like(run_scr)

    mix = jnp.dot(y_ref[...], wmix_ref[...], preferred_element_type=F32)
    x1 = _layer_norm(alpha * x_ref[...] + g1_ref[...] * mix, lng_ref[...], lnb_ref[...])
    x1_ref[...] = x1
    v = x1 * (1.0 + sc2_ref[...]) + sh2_ref[...]
    tm, d = v.shape
    _store_token_rows(vp_ref, _pack_bf16_pair(v[:, : d // 2], v[:, d // 2:]))

    v_hi = v.astype(BF16)
    v_lo = (v - v_hi.astype(F32)).astype(BF16)
    hh_hl = jnp.dot(v_hi, rw2_ref[...], preferred_element_type=F32)
    lh = jnp.dot(v_lo, rwh_ref[...], preferred_element_type=F32)
    logits = hh_hl[:, :n_experts] + hh_hl[:, n_experts:] + lh + rb_ref[...]
    lane = lax.broadcasted_iota(I32, (tm, n_experts), 1).astype(F32)
    work = logits
    sel, val = [], []
    for _ in range(TOP_K):
        m = jnp.max(work, axis=-1, keepdims=True)
        s = jnp.min(jnp.where(work == m, lane, float(n_experts)), axis=-1, keepdims=True)
        sel.append(s)
        val.append(m)
        work = jnp.where(lane == s, -jnp.inf, work)
    ex = [jnp.exp(v_k - val[0]) for v_k in val]
    den = ex[0] + ex[1] + ex[2] + ex[3]

    onehot = jnp.zeros((tm, n_experts), F32)
    for s in sel:
        onehot = onehot + jnp.where(lane == s, 1.0, 0.0)
    rr = lax.broadcasted_iota(I32, (tm, tm), 0)
    cc = lax.broadcasted_iota(I32, (tm, tm), 1)
    tri = jnp.where(cc <= rr, 1.0, 0.0).astype(BF16)
    incl = jnp.dot(tri, onehot.astype(BF16), preferred_element_type=F32)
    before = run_scr[...] + incl - onehot

    mlane = lax.broadcasted_iota(I32, (tm, LANES), 1)
    meta = jnp.zeros((tm, LANES), F32)
    for k in range(TOP_K):
        pos_k = jnp.sum(jnp.where(lane == sel[k], before, 0.0), axis=-1, keepdims=True)
        meta = jnp.where(mlane == META_IDX + k, sel[k], meta)
        meta = jnp.where(mlane == META_GATE + k, ex[k] / den, meta)
        meta = jnp.where(mlane == META_POS + k, pos_k, meta)
    meta_ref[...] = meta
    run_scr[...] = run_scr[...] + incl[tm - 1:tm, :]
    cnt_ref[...] = run_scr[...]


def _mix_router(rows, y, w_mix, layer, x, g1, ln_g, ln_b, sc2, sh2, router_w, router_b, alpha):
    d = x.shape[1]
    tm = rows.tm
    e = router_w.shape[1]
    n_chunks = d // 2 // LANES
    rw_hi = router_w.astype(BF16)
    rw_lo = (router_w - rw_hi.astype(F32)).astype(BF16)
    mod_spec = pl.BlockSpec((None, 1, d), lambda i: (rows.mod_row(i), 0, 0))
    vec_spec = pl.BlockSpec((1, d), lambda i: (0, 0))
    return pl.pallas_call(
        functools.partial(_mix_router_kernel, alpha=alpha, n_experts=e),
        grid=(rows.n_tiles,),
        in_specs=[
            pl.BlockSpec((tm, d), lambda i: (i, 0)),
            pl.BlockSpec((None, d, d), lambda i: (layer, 0, 0)),
            pl.BlockSpec((tm, d), lambda i: (i, 0)),
            mod_spec, vec_spec, vec_spec, mod_spec, mod_spec,
            pl.BlockSpec((d, 2 * e), lambda i: (0, 0)),
            pl.BlockSpec((d, e), lambda i: (0, 0)),
            pl.BlockSpec((1, e), lambda i: (0, 0)),
        ],
        out_specs=[
            pl.BlockSpec((tm, d), lambda i: (i, 0)),
            pl.BlockSpec((tm * n_chunks, LANES), lambda i: (i, 0)),
            pl.BlockSpec((tm, LANES), lambda i: (i, 0)),
            pl.BlockSpec((1, e), lambda i: (0, 0)),
        ],
        out_shape=[
            jax.ShapeDtypeStruct((rows.n_rows, d), F32),
            jax.ShapeDtypeStruct((rows.n_rows * n_chunks, LANES), I32),
            jax.ShapeDtypeStruct((rows.n_rows, LANES), F32),
            jax.ShapeDtypeStruct((1, e), F32),
        ],
        scratch_shapes=[pltpu.VMEM((1, e), F32)],
        compiler_params=_params(("arbitrary",)),
        name="mix_ln_router",
    )(y, w_mix, x, g1, ln_g.reshape(1, d), ln_b.reshape(1, d), sc2, sh2,
      jnp.concatenate([rw_hi, rw_lo], axis=1), rw_hi, router_b.reshape(1, e))


def _dispatch_kernel(pad_lo_ref, pad_hi_ref, dest_ref, vp_ref, xg_ref, zero_scr, sem, zero_sem, *, tm, n_chunks):
    def slot_rows(ref, s):
        return ref.at[pl.ds(pl.multiple_of(s * n_chunks, n_chunks), n_chunks), :]

    @pl.when(pl.program_id(0) == 0)
    def _():
        zero_scr[...] = jnp.zeros_like(zero_scr)

        def zero_copy(s):
            return pltpu.make_async_copy(zero_scr, slot_rows(xg_ref, s), zero_sem)

        def start(s, carry):
            zero_copy(s).start()
            return carry

        def wait(s, carry):
            zero_copy(s).wait()
            return carry

        for e in range(pad_lo_ref.shape[0]):
            lax.fori_loop(pad_lo_ref[e], pad_hi_ref[e], start, 0)
        for e in range(pad_lo_ref.shape[0]):
            lax.fori_loop(pad_lo_ref[e], pad_hi_ref[e], wait, 0)

    def row_copy(r, d):
        return pltpu.make_async_copy(slot_rows(vp_ref, r), slot_rows(xg_ref, d), sem)

    def issue(r, carry):
        for k in range(TOP_K):
            row_copy(r, dest_ref[r * TOP_K + k]).start(priority=k % 2)
        return carry

    lax.fori_loop(0, tm, issue, 0, unroll=DMA_UNROLL)

    def drain(r, carry):
        row_copy(0, 0).wait()
        return carry

    lax.fori_loop(0, tm * TOP_K, drain, 0, unroll=DMA_UNROLL)


def _dispatch(vp, dest_flat, pad_lo, pad_hi, n_slots, tm, n_chunks):
    n_rows = vp.shape[0] // n_chunks
    grid_spec = pltpu.PrefetchScalarGridSpec(
        num_scalar_prefetch=2,
        grid=(n_rows // tm,),
        in_specs=[
            pl.BlockSpec((tm * TOP_K,), lambda i, lo, hi: (i,), memory_space=pltpu.SMEM),
            pl.BlockSpec((tm * n_chunks, LANES), lambda i, lo, hi: (i, 0)),
        ],
        out_specs=pl.BlockSpec(memory_space=pl.ANY),
        scratch_shapes=[pltpu.VMEM((n_chunks, LANES), I32), pltpu.SemaphoreType.DMA(()), pltpu.SemaphoreType.DMA(())],
    )
    return pl.pallas_call(
        functools.partial(_dispatch_kernel, tm=tm, n_chunks=n_chunks),
        grid_spec=grid_spec,
        out_shape=jax.ShapeDtypeStruct((n_slots * n_chunks, LANES), I32),
        compiler_params=_params(("arbitrary",)),
        name="dispatch",
    )(pad_lo, pad_hi, dest_flat, vp)


def _expert_kernel(be_ref, nused_ref, xg_ref, wup_ref, bup_ref, wdn_ref, bdn_ref, y_ref, x_scr):
    del be_ref
    i = pl.program_id(0)

    @pl.when(i < nused_ref[0])
    def _():
        bm, d = x_scr.shape
        half = d // 2
        n_chunks = half // LANES
        for c in range(n_chunks):
            lo, hi = _unpack_bf16_pair(_load_token_chunk(xg_ref, (), c, bm, n_chunks))
            x_scr[:, c * LANES:(c + 1) * LANES] = lo.astype(BF16)
            x_scr[:, half + c * LANES:half + (c + 1) * LANES] = hi.astype(BF16)
        h = jnp.dot(x_scr[...], wup_ref[...], preferred_element_type=F32) + bup_ref[...]
        f = h.shape[1] // 2
        glu = jnp.minimum(h[:, :f], SWIGLU_LIMIT)
        lin = jnp.clip(h[:, f:], -SWIGLU_LIMIT, SWIGLU_LIMIT)
        act = glu * _sigmoid(SWIGLU_ALPHA * glu) * (lin + 1.0)
        y = jnp.dot(act.astype(BF16), wdn_ref[...], preferred_element_type=F32) + bdn_ref[...]
        _store_token_rows(y_ref, _pack_bf16_pair(y[:, :half], y[:, half:]))

    @pl.when(i >= nused_ref[0])
    def _():
        y_ref[...] = jnp.zeros_like(y_ref)


def _experts(xg, block_e, n_used, w_up, b_up, w_down, b_down, layer):
    _, e, d, f2 = w_up.shape
    n_chunks = d // 2 // LANES
    bm = EXPERT_ROWS
    n_blocks = xg.shape[0] // n_chunks // bm

    def x_map(i, be, nu):
        return (jnp.maximum(jnp.minimum(i, nu[0] - 1), 0), 0)

    def w_map(i, be, nu):
        return (0, be[i], 0, 0)

    def b_map(i, be, nu):
        return (layer, be[i], 0, 0)

    grid_spec = pltpu.PrefetchScalarGridSpec(
        num_scalar_prefetch=2,
        grid=(n_blocks,),
        in_specs=[
            pl.BlockSpec((bm * n_chunks, LANES), x_map),
            pl.BlockSpec((None, None, d, f2), w_map),
            pl.BlockSpec((None, None, 1, f2), b_map),
            pl.BlockSpec((None, None, f2 // 2, d), w_map),
            pl.BlockSpec((None, None, 1, d), b_map),
        ],
        out_specs=pl.BlockSpec((bm * n_chunks, LANES), lambda i, be, nu: (i, 0)),
        scratch_shapes=[pltpu.VMEM((bm, d), BF16)],
    )
    depth = b_up.shape[0]
    return pl.pallas_call(
        _expert_kernel,
        grid_spec=grid_spec,
        out_shape=jax.ShapeDtypeStruct(xg.shape, I32),
        compiler_params=_params(("arbitrary",)),
        name="experts",
    )(block_e, n_used, xg, w_up, b_up.reshape(depth, e, 1, f2), w_down, b_down.reshape(depth, e, 1, d))


def _combine_kernel(dcur_ref, dnext_ref, meta_ref, x1_ref, g2_ref, lng_ref, lnb_ref, y_hbm,
                    o_ref, buf, f_scr, sem, *, alpha):
    i = pl.program_id(0)
    n = pl.num_programs(0)
    tm, d = x1_ref.shape
    half = d // 2
    n_chunks = half // LANES

    def row_copy(d_slot, slot, r, k):
        src = y_hbm.at[pl.ds(pl.multiple_of(d_slot * n_chunks, n_chunks), n_chunks), :]
        dst = buf.at[slot, k, pl.ds(pl.multiple_of(r * n_chunks, n_chunks), n_chunks), :]
        return pltpu.make_async_copy(src, dst, sem.at[slot])

    def issue_tile(dref, slot):
        def body(r, carry):
            for k in range(TOP_K):
                row_copy(dref[r * TOP_K + k], slot, r, k).start(priority=k % 2)
            return carry
        lax.fori_loop(0, tm, body, 0, unroll=DMA_UNROLL)

    @pl.when(i == 0)
    def _():
        issue_tile(dcur_ref, 0)

    @pl.when(i + 1 < n)
    def _():
        issue_tile(dnext_ref, (i + 1) % 2)

    slot = i % 2

    def drain(r, carry):
        row_copy(0, slot, 0, 0).wait()
        return carry

    lax.fori_loop(0, tm * TOP_K, drain, 0, unroll=DMA_UNROLL)

    meta = meta_ref[...]
    gate = [meta[:, META_GATE + k:META_GATE + k + 1] for k in range(TOP_K)]
    for c in range(n_chunks):
        f_lo = jnp.zeros((tm, LANES), F32)
        f_hi = jnp.zeros((tm, LANES), F32)
        for k in range(TOP_K):
            lo, hi = _unpack_bf16_pair(_load_token_chunk(buf, (slot, k), c, tm, n_chunks))
            f_lo = f_lo + gate[k] * lo
            f_hi = f_hi + gate[k] * hi
        f_scr[:, c * LANES:(c + 1) * LANES] = f_lo
        f_scr[:, half + c * LANES:half + (c + 1) * LANES] = f_hi
    o_ref[...] = _layer_norm(alpha * x1_ref[...] + g2_ref[...] * f_scr[...], lng_ref[...], lnb_ref[...])


def _combine(rows, dest_flat, meta, x1, g2, ln_g, ln_b, y_slots, alpha):
    d = x1.shape[1]
    tm = rows.tm
    n_chunks = d // 2 // LANES
    last = rows.n_tiles - 1
    vec_spec = pl.BlockSpec((1, d), lambda i: (0, 0))
    return pl.pallas_call(
        functools.partial(_combine_kernel, alpha=alpha),
        grid=(rows.n_tiles,),
        in_specs=[
            pl.BlockSpec((tm * TOP_K,), lambda i: (i,), memory_space=pltpu.SMEM),
            pl.BlockSpec((tm * TOP_K,), lambda i: (jnp.minimum(i + 1, last),), memory_space=pltpu.SMEM),
            pl.BlockSpec((tm, LANES), lambda i: (i, 0)),
            pl.BlockSpec((tm, d), lambda i: (i, 0)),
            pl.BlockSpec((None, 1, d), lambda i: (rows.mod_row(i), 0, 0)),
            vec_spec, vec_spec,
            pl.BlockSpec(memory_space=pl.ANY),
        ],
        out_specs=pl.BlockSpec((tm, d), lambda i: (i, 0)),
        out_shape=jax.ShapeDtypeStruct((rows.n_rows, d), F32),
        scratch_shapes=[pltpu.VMEM((2, TOP_K, tm * n_chunks, LANES), I32), pltpu.VMEM((tm, d), F32),
                        pltpu.SemaphoreType.DMA((2,))],
        compiler_params=_params(("arbitrary",)),
        name="combine_ln",
    )(dest_flat, dest_flat, meta, x1, g2, ln_g.reshape(1, d), ln_b.reshape(1, d), y_slots)


def _slot_layout(meta, counts, n_rows):
    e = counts.shape[1]
    bm = EXPERT_ROWS
    cnt = counts[0].astype(I32)
    padded = (cnt + bm - 1) // bm * bm
    pend = jnp.cumsum(padded)
    pstart = pend - padded
    idx = meta[:, META_IDX:META_IDX + TOP_K].astype(I32)
    pos = meta[:, META_POS:META_POS + TOP_K].astype(I32)
    onehot = idx[..., None] == jnp.arange(e, dtype=I32)
    dest = jnp.sum(jnp.where(onehot, pstart, 0), axis=-1) + pos
    n_blocks = -(-(n_rows * TOP_K + e * (bm - 1)) // bm)
    n_used = pend[-1] // bm
    blk = jnp.minimum(jnp.arange(n_blocks, dtype=I32), n_used - 1) * bm
    block_e = jnp.minimum(jnp.sum(pend[None, :] <= blk[:, None], axis=-1), e - 1).astype(I32)
    pad_lo = jnp.concatenate([pstart + cnt, pend[-1:]]).astype(I32)
    pad_hi = jnp.concatenate([pend, jnp.full((1,), n_blocks * bm, I32)]).astype(I32)
    return dest.reshape(-1), block_e, n_used.reshape(1).astype(I32), pad_lo, pad_hi, n_blocks * bm


def _rope_tables(seq, pad_rows):
    rows = seq // GRID_W
    row = jnp.repeat(jnp.arange(rows, dtype=I32), GRID_W).astype(F32)
    col = jnp.tile(jnp.arange(GRID_W, dtype=I32), rows).astype(F32)
    half = HEAD_DIM // 2
    inv = ROPE_THETA ** (-jnp.arange(0, half, 2, dtype=F32) / half)
    ang = jnp.concatenate([row[:, None] * inv, col[:, None] * inv], axis=-1)
    cos = jnp.repeat(jnp.cos(ang), 2, axis=-1)
    sin = jnp.repeat(jnp.sin(ang), 2, axis=-1) * jnp.tile(jnp.array([-1.0, 1.0], F32), half)
    cos = jnp.concatenate([cos, jnp.ones((pad_rows, HEAD_DIM), F32)], axis=0)
    sin = jnp.concatenate([sin, jnp.zeros((pad_rows, HEAD_DIM), F32)], axis=0)
    return cos, sin


def kernel(x, c, ctx, c_ctx, ada_w, ada_b, w_in, q_norm_g, k_norm_g, conv_w, w_attn_o, w_conv_o, w_mix_o,
           ln1_g, ln1_b, router_w, router_b, w_up, b_up, w_down, b_down, ln2_g, ln2_b):
    batch, seq, d = x.shape
    ctx_len = ctx.shape[1]
    depth = ada_w.shape[0]
    n_q = d // HEAD_DIM
    d_q, d_kv, d_conv = d, n_q // GQA_GROUP * HEAD_DIM, d
    o_k, o_v = d_q, d_q + d_kv
    o_cb = o_v + d_kv
    o_cc, o_cx = o_cb + d_conv, o_cb + 2 * d_conv
    o_ga = o_cx + d_conv
    n_lat, n_ctx = batch * seq, batch * ctx_len
    n_tok = n_lat + n_ctx
    alpha = (2 * depth) ** 0.25
    assert seq & (seq - 1) == 0 and ctx_len & (ctx_len - 1) == 0 and seq % ctx_len == 0 and seq % GRID_W == 0

    tn = next(t for t in (512, 256, 128) if all(o % t == 0 for o in (o_k, o_v, o_cb, o_cc, o_cx, o_ga)))
    row_unit = math.gcd(seq, n_ctx)
    tm_proj, tm_merge, tm_small = _tile(row_unit, 1024), _tile(row_unit, 1024), _tile(row_unit, 256)

    mod_rows = -(-(batch + 1) // 8) * 8
    cvec = jnp.zeros((mod_rows, d), F32).at[:batch].set(c).at[batch].set(c_ctx)
    mod = _adaln(cvec, ada_w, ada_b).reshape(depth, mod_rows, 6, 1, d)

    cos, sin = _rope_tables(seq, tm_proj)
    scale = HEAD_DIM ** -0.5
    state = jnp.concatenate([x.reshape(n_lat, d), ctx.reshape(n_ctx, d)], axis=0)
    n_chunks = d // 2 // LANES
    w_in_b, w_attn_b, w_conv_b, w_mix_b = (w.astype(BF16) for w in (w_in, w_attn_o, w_conv_o, w_mix_o))
    pair = jnp.arange(HEAD_DIM) ^ 1
    n_vt, n_vcb = d_kv // tn, (d_kv + d_conv) // tn

    for l in range(depth):
        last = l == depth - 1
        sh1, sc1, g1, sh2, sc2, g2 = [mod[l, :, m] for m in range(6)]
        gq, gk = q_norm_g[l] * scale, k_norm_g[l]
        tab_a = jnp.stack([cos * gq, cos * gk])
        tab_b = jnp.stack([sin * gq[pair], sin * gk[pair]])
        n_main = n_lat if last else n_tok

        rows_main = _Rows(n_main, tm_proj, batch, seq, n_lat)
        qk, vcb, z, gates, w_up_b, w_down_b = _proj_fused(rows_main, state, sc1, sh1, w_in_b, l, tab_a, tab_b,
                                                          d_q, d_kv, d_conv, tn, w_up, w_down)
        if last:
            rows_ctx = _Rows(n_ctx, tm_proj, batch, seq, n_lat, tile0=n_lat // tm_proj)
            qk_ctx = _proj("qk", rows_ctx, state, sc1, sh1, w_in_b, l, 0, d_q + d_kv, tn,
                           extra=(tab_a, tab_b, d_q // tn), name="proj_qk_ctx")
            vcb_ctx = _proj("plain", rows_ctx, state, sc1, sh1, w_in_b, l, o_v, d_kv + d_conv, tn,
                            out_map=lambda i, j: (i, (j + n_vcb - n_vt) % n_vcb), name="proj_vcb_ctx")
            ctx_blk0 = 0
        else:
            qk_ctx, vcb_ctx, ctx_blk0 = qk, vcb, n_lat // ctx_len

        attn = _attention(qk, vcb, qk_ctx, vcb_ctx, ctx_blk0, batch, seq, ctx_len, d_q, d_conv, ctx_queries=not last)

        rows_m = _Rows(n_main, tm_merge, batch, seq, n_lat)
        y = _merge(rows_m, attn, vcb, z, gates, conv_w, w_attn_b, w_conv_b, l, seq, ctx_len)

        rows_s = _Rows(n_main, tm_small, batch, seq, n_lat)
        x1, vp, meta, counts = _mix_router(rows_s, y, w_mix_b, l, state, g1, ln1_g[l], ln1_b[l],
                                           sc2, sh2, router_w[l], router_b[l], alpha)
        dest, block_e, n_used, pad_lo, pad_hi, n_slots = _slot_layout(meta, counts, n_main)
        xg = _dispatch(vp, dest, pad_lo, pad_hi, n_slots, tm_small, n_chunks)
        y_slots = _experts(xg, block_e, n_used, w_up_b, b_up, w_down_b, b_down, l)
        state = _combine(rows_s, dest, meta, x1, g2, ln2_g[l], ln2_b[l], y_slots, alpha)

    return state[:n_lat].reshape(batch, seq, d)
```

```python
import functools
import math

import jax
import jax.numpy as jnp
from jax import lax
from jax.experimental import pallas as pl
from jax.experimental.pallas import tpu as pltpu

F32 = jnp.float32
BF16 = jnp.bfloat16
I32 = jnp.int32

LANES = 128
HEAD_DIM = 128
GQA_GROUP = 4
GRID_W = 64
TOP_K = 4
SWIGLU_ALPHA = 1.702
SWIGLU_LIMIT = 7.0
ROPE_THETA = 10000.0
NORM_EPS = 1e-6
VMEM_LIMIT = 56 * 1024 * 1024
EXPERT_ROWS = 256
META_IDX, META_GATE, META_POS = 0, 4, 8
DMA_UNROLL = 8
ATTN_Q_ROWS = 1024


def _params(sem, vmem=VMEM_LIMIT):
    return pltpu.CompilerParams(dimension_semantics=sem, vmem_limit_bytes=vmem)


def _tile(n, pref):
    if n <= pref:
        return n
    t = pref - pref % 8
    while n % t:
        t -= 8
    return t


def _adaln_kernel(c_ref, w_ref, b_ref, o_ref):
    c = c_ref[...]
    s = (c * jax.nn.sigmoid(c)).astype(BF16)
    o_ref[...] = jnp.dot(s, w_ref[...].astype(BF16), preferred_element_type=F32) + b_ref[...]


def _adaln(cvec, ada_w, ada_b):
    depth, d, n = ada_w.shape
    r = cvec.shape[0]
    tn = _tile(n, 1024)
    return pl.pallas_call(
        _adaln_kernel,
        grid=(depth, n // tn),
        in_specs=[
            pl.BlockSpec((r, d), lambda l, j: (0, 0)),
            pl.BlockSpec((None, d, tn), lambda l, j: (l, 0, j)),
            pl.BlockSpec((None, 1, tn), lambda l, j: (l, 0, j)),
        ],
        out_specs=pl.BlockSpec((None, r, tn), lambda l, j: (l, 0, j)),
        out_shape=jax.ShapeDtypeStruct((depth, r, n), F32),
        compiler_params=_params(("parallel", "parallel")),
        name="adaln",
    )(cvec, ada_w, ada_b.reshape(depth, 1, n))


def _modulate(u_scr, x_ref, sc_ref, sh_ref):
    @pl.when(pl.program_id(1) == 0)
    def _():
        u_scr[...] = (x_ref[...] * (1.0 + sc_ref[...]) + sh_ref[...]).astype(BF16)


def _sigmoid(x):
    return 0.5 * jnp.tanh(0.5 * x) + 0.5


def _norm_rope_store(acc, ta, tb, o_ref):
    tn = acc.shape[1]
    rr = lax.broadcasted_iota(I32, (HEAD_DIM, HEAD_DIM), 0)
    cc = lax.broadcasted_iota(I32, (HEAD_DIM, HEAD_DIM), 1)
    swap = jnp.where((rr ^ 1) == cc, 1.0, 0.0).astype(BF16)
    mean_w = jnp.full((HEAD_DIM, HEAD_DIM), 1.0 / HEAD_DIM, BF16)
    for h in range(tn // HEAD_DIM):
        cols = slice(h * HEAD_DIM, (h + 1) * HEAD_DIM)
        a = acc[:, cols]
        ms = jnp.dot((a * a).astype(BF16), mean_w, preferred_element_type=F32)
        partner = jnp.dot(a.astype(BF16), swap, preferred_element_type=F32)
        o_ref[:, cols] = (lax.rsqrt(ms + NORM_EPS) * (a * ta + partner * tb)).astype(BF16)


def _proj_qk_kernel(x_ref, sc_ref, sh_ref, w_ref, ta_ref, tb_ref, o_ref, u_scr):
    _modulate(u_scr, x_ref, sc_ref, sh_ref)
    acc = jnp.dot(u_scr[...], w_ref[...], preferred_element_type=F32)
    _norm_rope_store(acc, ta_ref[...], tb_ref[...], o_ref)


def _proj_plain_kernel(x_ref, sc_ref, sh_ref, w_ref, o_ref, u_scr):
    _modulate(u_scr, x_ref, sc_ref, sh_ref)
    o_ref[...] = jnp.dot(u_scr[...], w_ref[...], preferred_element_type=F32).astype(BF16)


def _proj_fused_kernel(x_ref, sc_ref, sh_ref, w_ref, wx_ref, ta_ref, tb_ref, wu_ref, wd_ref,
                       qk_ref, vcb_ref, z_ref, g_ref, wub_ref, wdb_ref, u_scr, *, n_qk, n_vcb, n_z, cast_steps):
    _modulate(u_scr, x_ref, sc_ref, sh_ref)
    j = pl.program_id(1)
    z0 = n_qk + n_vcb

    @pl.when(pl.program_id(0) * pl.num_programs(1) + j < cast_steps)
    def _():
        wub_ref[...] = wu_ref[...].astype(BF16)
        wdb_ref[...] = wd_ref[...].astype(BF16)

    @pl.when(j < n_qk)
    def _():
        acc = jnp.dot(u_scr[...], w_ref[...], preferred_element_type=F32)
        _norm_rope_store(acc, ta_ref[...], tb_ref[...], qk_ref)

    @pl.when((j >= n_qk) & (j < z0))
    def _():
        vcb_ref[...] = jnp.dot(u_scr[...], w_ref[...], preferred_element_type=F32).astype(BF16)

    @pl.when((j >= z0) & (j < z0 + n_z))
    def _():
        u = u_scr[...]
        c = jnp.dot(u, w_ref[...], preferred_element_type=F32)
        xin = jnp.dot(u, wx_ref[...], preferred_element_type=F32)
        z_ref[...] = (c * xin).astype(BF16)

    @pl.when(j >= z0 + n_z)
    def _():
        acc = jnp.dot(u_scr[...], w_ref[...], preferred_element_type=F32)
        g_ref[...] = _sigmoid(acc).astype(BF16)


class _Rows:
    def __init__(self, n_rows, tm, batch, seq, n_lat, tile0=0):
        assert n_rows % tm == 0 and seq % tm == 0 and n_lat % tm == 0
        self.n_rows, self.tm, self.batch, self.tile0 = n_rows, tm, batch, tile0
        self.n_tiles = n_rows // tm
        self.lat_tiles = n_lat // tm
        self.per_seq = seq // tm

    def state_tile(self, i):
        return i + self.tile0

    def mod_row(self, i):
        t = i + self.tile0
        return jnp.where(t < self.lat_tiles, t // self.per_seq, self.batch)

    def rope_blk(self, i):
        t = i + self.tile0
        return jnp.where(t < self.lat_tiles, t % self.per_seq, self.per_seq)


def _proj_common_specs(rows, d):
    return [
        pl.BlockSpec((rows.tm, d), lambda i, j: (rows.state_tile(i), 0)),
        pl.BlockSpec((None, 1, d), lambda i, j: (rows.mod_row(i), 0, 0)),
        pl.BlockSpec((None, 1, d), lambda i, j: (rows.mod_row(i), 0, 0)),
    ]


def _proj(kind, rows, x, sc, sh, w, layer, col_off, n_cols, tn, extra=(), out_map=None, name="proj"):
    d = x.shape[1]
    tm = rows.tm
    assert col_off % tn == 0
    off = col_off // tn
    in_specs = _proj_common_specs(rows, d) + [pl.BlockSpec((None, d, tn), lambda i, j: (layer, 0, off + j))]
    operands = [x, sc, sh, w]
    if kind == "qk":
        tab_a, tab_b, q_tiles = extra
        tab_spec = pl.BlockSpec((None, tm, HEAD_DIM), lambda i, j: (jnp.where(j < q_tiles, 0, 1), rows.rope_blk(i), 0))
        in_specs += [tab_spec, tab_spec]
        operands += [tab_a, tab_b]
    if out_map is None:
        out_map = lambda i, j: (i, j)
    return pl.pallas_call(
        {"qk": _proj_qk_kernel, "plain": _proj_plain_kernel}[kind],
        grid=(rows.n_tiles, n_cols // tn),
        in_specs=in_specs,
        out_specs=pl.BlockSpec((tm, tn), out_map),
        out_shape=jax.ShapeDtypeStruct((rows.n_rows, n_cols), BF16),
        scratch_shapes=[pltpu.VMEM((tm, d), BF16)],
        compiler_params=_params(("parallel", "arbitrary")),
        name=name,
    )(*operands)


def _proj_fused(rows, x, sc, sh, w, layer, tab_a, tab_b, d_q, d_kv, d_conv, tn, w_up, w_down):
    d = x.shape[1]
    tm = rows.tm
    n_qk, n_vt, n_vcb, n_z, n_g = (d_q + d_kv) // tn, d_kv // tn, (d_kv + d_conv) // tn, d_conv // tn, 2 * d // tn
    q_tiles = d_q // tn
    z0 = n_qk + n_vcb
    n_j = n_qk + n_vcb + n_z + n_g
    depth, e, _, f2 = w_up.shape
    cast_steps = 1 << ((rows.n_tiles * n_j).bit_length() - 1)
    up_rows, dn_rows = e * d // cast_steps, e * (f2 // 2) // cast_steps
    assert up_rows % 16 == 0 and dn_rows % 16 == 0

    def clamp(v, n):
        return jnp.clip(v, 0, n - 1)

    def cast_blk(i, j):
        return jnp.minimum(i * n_j + j, cast_steps - 1)

    tab_spec = pl.BlockSpec((None, tm, HEAD_DIM), lambda i, j: (jnp.where(j < q_tiles, 0, 1), rows.rope_blk(i), 0))
    in_specs = _proj_common_specs(rows, d) + [
        pl.BlockSpec((None, d, tn), lambda i, j: (layer, 0, j + jnp.where(j >= z0 + n_z, n_z, 0))),
        pl.BlockSpec((None, d, tn), lambda i, j: (layer, 0, z0 + n_z + clamp(j - z0, n_z))),
        tab_spec, tab_spec,
        pl.BlockSpec((up_rows, f2), lambda i, j: (layer * cast_steps + cast_blk(i, j), 0)),
        pl.BlockSpec((dn_rows, d), lambda i, j: (layer * cast_steps + cast_blk(i, j), 0)),
    ]
    out_specs = [
        pl.BlockSpec((tm, tn), lambda i, j: (i, clamp(j, n_qk))),
        pl.BlockSpec((tm, tn), lambda i, j: (i, (clamp(j - n_qk, n_vcb) + n_vcb - n_vt) % n_vcb)),
        pl.BlockSpec((tm, tn), lambda i, j: (i, clamp(j - z0, n_z))),
        pl.BlockSpec((tm, tn), lambda i, j: (i, clamp(j - z0 - n_z, n_g))),
        pl.BlockSpec((up_rows, f2), lambda i, j: (cast_blk(i, j), 0)),
        pl.BlockSpec((dn_rows, d), lambda i, j: (cast_blk(i, j), 0)),
    ]
    out_shape = [jax.ShapeDtypeStruct((rows.n_rows, n * tn), BF16) for n in (n_qk, n_vcb, n_z, n_g)]
    out_shape += [jax.ShapeDtypeStruct((e * d, f2), BF16), jax.ShapeDtypeStruct((e * (f2 // 2), d), BF16)]
    qk, vcb, z, gates, w_up_b, w_down_b = pl.pallas_call(
        functools.partial(_proj_fused_kernel, n_qk=n_qk, n_vcb=n_vcb, n_z=n_z, cast_steps=cast_steps),
        grid=(rows.n_tiles, n_j),
        in_specs=in_specs,
        out_specs=out_specs,
        out_shape=out_shape,
        scratch_shapes=[pltpu.VMEM((tm, d), BF16)],
        compiler_params=_params(("arbitrary", "arbitrary")),
        name="proj_fused",
    )(x, sc, sh, w, w, tab_a, tab_b, w_up.reshape(depth * e * d, f2), w_down.reshape(depth * e * (f2 // 2), d))
    return qk, vcb, z, gates, w_up_b.reshape(1, e, d, f2), w_down_b.reshape(1, e, f2 // 2, d)


def _attn_heads(q_ref, kl_ref, kc_ref, vl_ext, vc_ext, o_ref, with_latent):
    nt = (((1,), (1,)), ((), ()))
    unit = kc_ref.shape[0]
    for r0 in range(0, q_ref.shape[0], unit):
        rows = slice(r0, r0 + unit)
        for h in range(GQA_GROUP):
            cols = slice(h * HEAD_DIM, (h + 1) * HEAD_DIM)
            q = q_ref[rows, cols]
            s_c = lax.dot_general(q, kc_ref[...], nt, preferred_element_type=F32)
            m = jnp.max(s_c, axis=-1, keepdims=True)
            if with_latent:
                s_l = lax.dot_general(q, kl_ref[...], nt, preferred_element_type=F32)
                m = jnp.maximum(m, jnp.max(s_l, axis=-1, keepdims=True))
            p_c = jnp.exp((s_c - m).astype(BF16))
            o = jnp.dot(p_c, vc_ext[...], preferred_element_type=F32)
            if with_latent:
                p_l = jnp.exp((s_l - m).astype(BF16))
                o = o + jnp.dot(p_l, vl_ext[...], preferred_element_type=F32)
            o_ref[rows, cols] = (o[:, :HEAD_DIM] / o[:, HEAD_DIM:]).astype(BF16)


def _attn_kernel(q_ref, kl_ref, vl_ref, kc_ref, vc_ref, o_ref, vl_ext, vc_ext, *, lat_steps, ctx_step):
    qi = pl.program_id(2)

    @pl.when(qi == 0)
    def _():
        vl_ext[:, :HEAD_DIM] = vl_ref[...]
        vl_ext[:, HEAD_DIM:] = jnp.ones_like(vl_ref)
        vc_ext[:, :HEAD_DIM] = vc_ref[...]
        vc_ext[:, HEAD_DIM:] = jnp.ones_like(vc_ref)

    if not ctx_step:
        _attn_heads(q_ref, kl_ref, kc_ref, vl_ext, vc_ext, o_ref, True)
        return

    @pl.when(qi < lat_steps)
    def _():
        _attn_heads(q_ref, kl_ref, kc_ref, vl_ext, vc_ext, o_ref, True)

    @pl.when(qi >= lat_steps)
    def _():
        _attn_heads(q_ref, kl_ref, kc_ref, vl_ext, vc_ext, o_ref, False)


def _attention(qk, vcb, qk_ctx, vcb_ctx, ctx_blk0, batch, seq, ctx_len, d_q, d_conv, ctx_queries):
    n_kv = d_q // HEAD_DIM // GQA_GROUP
    tq = ctx_len if ctx_queries else _tile(seq, ATTN_Q_ROWS)
    lat_steps = seq // tq
    gw = GQA_GROUP * HEAD_DIM
    k_col = d_q // HEAD_DIM
    v_col = d_conv // HEAD_DIM
    n_rows = batch * seq + (batch * ctx_len if ctx_queries else 0)

    def q_map(b, g, qi):
        return (jnp.where(qi < lat_steps, b * lat_steps + qi, batch * lat_steps + b), g)

    return pl.pallas_call(
        functools.partial(_attn_kernel, lat_steps=lat_steps, ctx_step=ctx_queries),
        grid=(batch, n_kv, lat_steps + (1 if ctx_queries else 0)),
        in_specs=[
            pl.BlockSpec((tq, gw), q_map),
            pl.BlockSpec((seq, HEAD_DIM), lambda b, g, qi: (b, k_col + g)),
            pl.BlockSpec((seq, HEAD_DIM), lambda b, g, qi: (b, v_col + g)),
            pl.BlockSpec((ctx_len, HEAD_DIM), lambda b, g, qi: (ctx_blk0 + b, k_col + g)),
            pl.BlockSpec((ctx_len, HEAD_DIM), lambda b, g, qi: (ctx_blk0 + b, v_col + g)),
        ],
        out_specs=pl.BlockSpec((tq, gw), q_map),
        out_shape=jax.ShapeDtypeStruct((n_rows, d_q), BF16),
        scratch_shapes=[pltpu.VMEM((seq, 2 * HEAD_DIM), BF16), pltpu.VMEM((ctx_len, 2 * HEAD_DIM), BF16)],
        compiler_params=_params(("parallel", "parallel", "arbitrary")),
        name="attention",
    )(qk, qk, vcb, qk_ctx, vcb_ctx)


def _merge_kernel(attn_ref, cb_ref, z_ref, zprev_ref, znext_ref, cw_ref, wa_ref, wc_ref, ga_ref, gc_ref,
                  o_ref, conv_scr, *, rows, seq, ctx_len, chunk):
    i = pl.program_id(0)

    @pl.when(pl.program_id(1) == 0)
    def _():
        tm, d = conv_scr.shape
        seq_len = jnp.where(i < rows.lat_tiles, seq, ctx_len)
        r = lax.broadcasted_iota(I32, (tm, chunk), 0)
        pos = (r + i * tm) & (seq_len - 1)
        first, last = pos == 0, pos == seq_len - 1
        for c0 in range(0, d, chunk):
            cols = slice(c0, c0 + chunk)
            z = z_ref[:, cols].astype(F32)
            z_prev = zprev_ref[:, cols].astype(F32)
            z_next = znext_ref[:, cols].astype(F32)
            before = jnp.where(r == 0, z_prev[7:8], pltpu.roll(z, 1, 0))
            after = jnp.where(r == tm - 1, z_next[0:1], pltpu.roll(z, tm - 1, 0))
            before = jnp.where(first, 0.0, before)
            after = jnp.where(last, 0.0, after)
            conv = cw_ref[0:1, cols] * before + cw_ref[1:2, cols] * z + cw_ref[2:3, cols] * after
            conv_scr[:, cols] = (cb_ref[:, cols].astype(F32) * conv).astype(BF16)

    a = jnp.dot(attn_ref[...], wa_ref[...], preferred_element_type=F32)
    c = jnp.dot(conv_scr[...], wc_ref[...], preferred_element_type=F32)
    o_ref[...] = (ga_ref[...].astype(F32) * a + gc_ref[...].astype(F32) * c).astype(BF16)


def _merge(rows, attn, vcb, z, gates, conv_w, w_attn_o, w_conv_o, layer, seq, ctx_len):
    d = attn.shape[1]
    tm = rows.tm
    tn = _tile(d, 512)
    n_j = d // tn
    halo = tm // 8
    last_halo = rows.n_rows // 8 - 1
    return pl.pallas_call(
        functools.partial(_merge_kernel, rows=rows, seq=seq, ctx_len=ctx_len, chunk=_tile(d, 512)),
        grid=(rows.n_tiles, n_j),
        in_specs=[
            pl.BlockSpec((tm, d), lambda i, j: (i, 0)),
            pl.BlockSpec((tm, d), lambda i, j: (i, 0)),
            pl.BlockSpec((tm, d), lambda i, j: (i, 0)),
            pl.BlockSpec((8, d), lambda i, j: (jnp.maximum(i * halo - 1, 0), 0)),
            pl.BlockSpec((8, d), lambda i, j: (jnp.minimum((i + 1) * halo, last_halo), 0)),
            pl.BlockSpec((None, 3, d), lambda i, j: (layer, 0, 0)),
            pl.BlockSpec((None, d, tn), lambda i, j: (layer, 0, j)),
            pl.BlockSpec((None, d, tn), lambda i, j: (layer, 0, j)),
            pl.BlockSpec((tm, tn), lambda i, j: (i, j)),
            pl.BlockSpec((tm, tn), lambda i, j: (i, n_j + j)),
        ],
        out_specs=pl.BlockSpec((tm, tn), lambda i, j: (i, j)),
        out_shape=jax.ShapeDtypeStruct((rows.n_rows, d), BF16),
        scratch_shapes=[pltpu.VMEM((tm, d), BF16)],
        compiler_params=_params(("parallel", "arbitrary")),
        name="merge",
    )(attn, vcb, z, z, z, conv_w, w_attn_o, w_conv_o, gates, gates)


def _layer_norm(r, g, b):
    mu = jnp.mean(r, axis=-1, keepdims=True)
    c = r - mu
    var = jnp.mean(c * c, axis=-1, keepdims=True)
    return c * lax.rsqrt(var + NORM_EPS) * g + b


def _pack_bf16_pair(lo, hi):
    lo_bits = lax.bitcast_convert_type(lo.astype(BF16).astype(F32), I32)
    hi_bits = lax.bitcast_convert_type(hi.astype(BF16).astype(F32), I32)
    return lax.shift_right_logical(lo_bits, jnp.full_like(lo_bits, 16)) | (hi_bits & jnp.int32(-65536))


def _unpack_bf16_pair(w):
    lo = lax.bitcast_convert_type(lax.shift_left(w, jnp.full_like(w, 16)), F32)
    hi = lax.bitcast_convert_type(w & jnp.int32(-65536), F32)
    return lo, hi


def _store_token_rows(ref, packed):
    tm, half = packed.shape
    n_chunks = half // LANES
    for c in range(n_chunks):
        ref[pl.ds(c, tm, stride=n_chunks), :] = packed[:, c * LANES:(c + 1) * LANES]


def _load_token_chunk(ref, lead, c, tm, n_chunks):
    return ref[(*lead, pl.ds(c, tm, stride=n_chunks), slice(None))]


def _mix_router_kernel(y_ref, wmix_ref, x_ref, g1_ref, lng_ref, lnb_ref, sc2_ref, sh2_ref, rw2_ref, rwh_ref, rb_ref,
                       x1_ref, vp_ref, meta_ref, cnt_ref, run_scr, *, alpha, n_experts):
    @pl.when(pl.program_id(0) == 0)
    def _():
        run_scr[...] = jnp.zeros_like(run_scr)

    mix = jnp.dot(y_ref[...], wmix_ref[...], preferred_element_type=F32)
    x1 = _layer_norm(alpha * x_ref[...] + g1_ref[...] * mix, lng_ref[...], lnb_ref[...])
    x1_ref[...] = x1
    v = x1 * (1.0 + sc2_ref[...]) + sh2_ref[...]
    tm, d = v.shape
    _store_token_rows(vp_ref, _pack_bf16_pair(v[:, : d // 2], v[:, d // 2:]))

    v_hi = v.astype(BF16)
    v_lo = (v - v_hi.astype(F32)).astype(BF16)
    hh_hl = jnp.dot(v_hi, rw2_ref[...], preferred_element_type=F32)
    lh = jnp.dot(v_lo, rwh_ref[...], preferred_element_type=F32)
    logits = hh_hl[:, :n_experts] + hh_hl[:, n_experts:] + lh + rb_ref[...]
    lane = lax.broadcasted_iota(I32, (tm, n_experts), 1).astype(F32)
    work = logits
    sel, val = [], []
    for _ in range(TOP_K):
        m = jnp.max(work, axis=-1, keepdims=True)
        s = jnp.min(jnp.where(work == m, lane, float(n_experts)), axis=-1, keepdims=True)
        sel.append(s)
        val.append(m)
        work = jnp.where(lane == s, -jnp.inf, work)
    ex = [jnp.exp(v_k - val[0]) for v_k in val]
    den = ex[0] + ex[1] + ex[2] + ex[3]

    onehot = jnp.zeros((tm, n_experts), F32)
    for s in sel:
        onehot = onehot + jnp.where(lane == s, 1.0, 0.0)
    rr = lax.broadcasted_iota(I32, (tm, tm), 0)
    cc = lax.broadcasted_iota(I32, (tm, tm), 1)
    tri = jnp.where(cc <= rr, 1.0, 0.0).astype(BF16)
    incl = jnp.dot(tri, onehot.astype(BF16), preferred_element_type=F32)
    before = run_scr[...] + incl - onehot

    mlane = lax.broadcasted_iota(I32, (tm, LANES), 1)
    meta = jnp.zeros((tm, LANES), F32)
    for k in range(TOP_K):
        pos_k = jnp.sum(jnp.where(lane == sel[k], before, 0.0), axis=-1, keepdims=True)
        meta = jnp.where(mlane == META_IDX + k, sel[k], meta)
        meta = jnp.where(mlane == META_GATE + k, ex[k] / den, meta)
        meta = jnp.where(mlane == META_POS + k, pos_k, meta)
    meta_ref[...] = meta
    run_scr[...] = run_scr[...] + incl[tm - 1:tm, :]
    cnt_ref[...] = run_scr[...]


def _mix_router(rows, y, w_mix, layer, x, g1, ln_g, ln_b, sc2, sh2, router_w, router_b, alpha):
    d = x.shape[1]
    tm = rows.tm
    e = router_w.shape[1]
    n_chunks = d // 2 // LANES
    rw_hi = router_w.astype(BF16)
    rw_lo = (router_w - rw_hi.astype(F32)).astype(BF16)
    mod_spec = pl.BlockSpec((None, 1, d), lambda i: (rows.mod_row(i), 0, 0))
    vec_spec = pl.BlockSpec((1, d), lambda i: (0, 0))
    return pl.pallas_call(
        functools.partial(_mix_router_kernel, alpha=alpha, n_experts=e),
        grid=(rows.n_tiles,),
        in_specs=[
            pl.BlockSpec((tm, d), lambda i: (i, 0)),
            pl.BlockSpec((None, d, d), lambda i: (layer, 0, 0)),
            pl.BlockSpec((tm, d), lambda i: (i, 0)),
            mod_spec, vec_spec, vec_spec, mod_spec, mod_spec,
            pl.BlockSpec((d, 2 * e), lambda i: (0, 0)),
            pl.BlockSpec((d, e), lambda i: (0, 0)),
            pl.BlockSpec((1, e), lambda i: (0, 0)),
        ],
        out_specs=[
            pl.BlockSpec((tm, d), lambda i: (i, 0)),
            pl.BlockSpec((tm * n_chunks, LANES), lambda i: (i, 0)),
            pl.BlockSpec((tm, LANES), lambda i: (i, 0)),
            pl.BlockSpec((1, e), lambda i: (0, 0)),
        ],
        out_shape=[
            jax.ShapeDtypeStruct((rows.n_rows, d), F32),
            jax.ShapeDtypeStruct((rows.n_rows * n_chunks, LANES), I32),
            jax.ShapeDtypeStruct((rows.n_rows, LANES), F32),
            jax.ShapeDtypeStruct((1, e), F32),
        ],
        scratch_shapes=[pltpu.VMEM((1, e), F32)],
        compiler_params=_params(("arbitrary",)),
        name="mix_ln_router",
    )(y, w_mix, x, g1, ln_g.reshape(1, d), ln_b.reshape(1, d), sc2, sh2,
      jnp.concatenate([rw_hi, rw_lo], axis=1), rw_hi, router_b.reshape(1, e))


def _dispatch_kernel(pad_lo_ref, pad_hi_ref, dest_ref, vp_ref, xg_ref, zero_scr, sem, zero_sem, *, tm, n_chunks):
    def slot_rows(ref, s):
        return ref.at[pl.ds(pl.multiple_of(s * n_chunks, n_chunks), n_chunks), :]

    @pl.when(pl.program_id(0) == 0)
    def _():
        zero_scr[...] = jnp.zeros_like(zero_scr)

        def zero_copy(s):
            return pltpu.make_async_copy(zero_scr, slot_rows(xg_ref, s), zero_sem)

        def start(s, carry):
            zero_copy(s).start()
            return carry

        def wait(s, carry):
            zero_copy(s).wait()
            return carry

        for e in range(pad_lo_ref.shape[0]):
            lax.fori_loop(pad_lo_ref[e], pad_hi_ref[e], start, 0)
        for e in range(pad_lo_ref.shape[0]):
            lax.fori_loop(pad_lo_ref[e], pad_hi_ref[e], wait, 0)

    def row_copy(r, d):
        return pltpu.make_async_copy(slot_rows(vp_ref, r), slot_rows(xg_ref, d), sem)

    def issue(r, carry):
        for k in range(TOP_K):
            row_copy(r, dest_ref[r * TOP_K + k]).start(priority=k % 2)
        return carry

    lax.fori_loop(0, tm, issue, 0, unroll=DMA_UNROLL)

    def drain(r, carry):
        row_copy(0, 0).wait()
        return carry

    lax.fori_loop(0, tm * TOP_K, drain, 0, unroll=DMA_UNROLL)


def _dispatch(vp, dest_flat, pad_lo, pad_hi, n_slots, tm, n_chunks):
    n_rows = vp.shape[0] // n_chunks
    grid_spec = pltpu.PrefetchScalarGridSpec(
        num_scalar_prefetch=2,
        grid=(n_rows // tm,),
        in_specs=[
            pl.BlockSpec((tm * TOP_K,), lambda i, lo, hi: (i,), memory_space=pltpu.SMEM),
            pl.BlockSpec((tm * n_chunks, LANES), lambda i, lo, hi: (i, 0)),
        ],
        out_specs=pl.BlockSpec(memory_space=pl.ANY),
        scratch_shapes=[pltpu.VMEM((n_chunks, LANES), I32), pltpu.SemaphoreType.DMA(()), pltpu.SemaphoreType.DMA(())],
    )
    return pl.pallas_call(
        functools.partial(_dispatch_kernel, tm=tm, n_chunks=n_chunks),
        grid_spec=grid_spec,
        out_shape=jax.ShapeDtypeStruct((n_slots * n_chunks, LANES), I32),
        compiler_params=_params(("arbitrary",)),
        name="dispatch",
    )(pad_lo, pad_hi, dest_flat, vp)


def _expert_kernel(be_ref, nused_ref, xg_ref, wup_ref, bup_ref, wdn_ref, bdn_ref, y_ref, x_scr):
    del be_ref
    i = pl.program_id(0)

    @pl.when(i < nused_ref[0])
    def _():
        bm, d = x_scr.shape
        half = d // 2
        n_chunks = half // LANES
        for c in range(n_chunks):
            lo, hi = _unpack_bf16_pair(_load_token_chunk(xg_ref, (), c, bm, n_chunks))
            x_scr[:, c * LANES:(c + 1) * LANES] = lo.astype(BF16)
            x_scr[:, half + c * LANES:half + (c + 1) * LANES] = hi.astype(BF16)
        h = jnp.dot(x_scr[...], wup_ref[...], preferred_element_type=F32) + bup_ref[...]
        f = h.shape[1] // 2
        glu = jnp.minimum(h[:, :f], SWIGLU_LIMIT)
        lin = jnp.clip(h[:, f:], -SWIGLU_LIMIT, SWIGLU_LIMIT)
        act = glu * _sigmoid(SWIGLU_ALPHA * glu) * (lin + 1.0)
        y = jnp.dot(act.astype(BF16), wdn_ref[...], preferred_element_type=F32) + bdn_ref[...]
        _store_token_rows(y_ref, _pack_bf16_pair(y[:, :half], y[:, half:]))

    @pl.when(i >= nused_ref[0])
    def _():
        y_ref[...] = jnp.zeros_like(y_ref)


def _experts(xg, block_e, n_used, w_up, b_up, w_down, b_down, layer):
    _, e, d, f2 = w_up.shape
    n_chunks = d // 2 // LANES
    bm = EXPERT_ROWS
    n_blocks = xg.shape[0] // n_chunks // bm

    def x_map(i, be, nu):
        return (jnp.maximum(jnp.minimum(i, nu[0] - 1), 0), 0)

    def w_map(i, be, nu):
        return (0, be[i], 0, 0)

    def b_map(i, be, nu):
        return (layer, be[i], 0, 0)

    grid_spec = pltpu.PrefetchScalarGridSpec(
        num_scalar_prefetch=2,
        grid=(n_blocks,),
        in_specs=[
            pl.BlockSpec((bm * n_chunks, LANES), x_map),
            pl.BlockSpec((None, None, d, f2), w_map),
            pl.BlockSpec((None, None, 1, f2), b_map),
            pl.BlockSpec((None, None, f2 // 2, d), w_map),
            pl.BlockSpec((None, None, 1, d), b_map),
        ],
        out_specs=pl.BlockSpec((bm * n_chunks, LANES), lambda i, be, nu: (i, 0)),
        scratch_shapes=[pltpu.VMEM((bm, d), BF16)],
    )
    depth = b_up.shape[0]
    return pl.pallas_call(
        _expert_kernel,
        grid_spec=grid_spec,
        out_shape=jax.ShapeDtypeStruct(xg.shape, I32),
        compiler_params=_params(("arbitrary",)),
        name="experts",
    )(block_e, n_used, xg, w_up, b_up.reshape(depth, e, 1, f2), w_down, b_down.reshape(depth, e, 1, d))


def _combine_kernel(dcur_ref, dnext_ref, meta_ref, x1_ref, g2_ref, lng_ref, lnb_ref, y_hbm,
                    o_ref, buf, f_scr, sem, *, alpha):
    i = pl.program_id(0)
    n = pl.num_programs(0)
    tm, d = x1_ref.shape
    half = d // 2
    n_chunks = half // LANES

    def row_copy(d_slot, slot, r, k):
        src = y_hbm.at[pl.ds(pl.multiple_of(d_slot * n_chunks, n_chunks), n_chunks), :]
        dst = buf.at[slot, k, pl.ds(pl.multiple_of(r * n_chunks, n_chunks), n_chunks), :]
        return pltpu.make_async_copy(src, dst, sem.at[slot])

    def issue_tile(dref, slot):
        def body(r, carry):
            for k in range(TOP_K):
                row_copy(dref[r * TOP_K + k], slot, r, k).start(priority=k % 2)
            return carry
        lax.fori_loop(0, tm, body, 0, unroll=DMA_UNROLL)

    @pl.when(i == 0)
    def _():
        issue_tile(dcur_ref, 0)

    @pl.when(i + 1 < n)
    def _():
        issue_tile(dnext_ref, (i + 1) % 2)

    slot = i % 2

    def drain(r, carry):
        row_copy(0, slot, 0, 0).wait()
        return carry

    lax.fori_loop(0, tm * TOP_K, drain, 0, unroll=DMA_UNROLL)

    meta = meta_ref[...]
    gate = [meta[:, META_GATE + k:META_GATE + k + 1] for k in range(TOP_K)]
    for c in range(n_chunks):
        f_lo = jnp.zeros((tm, LANES), F32)
        f_hi = jnp.zeros((tm, LANES), F32)
        for k in range(TOP_K):
            lo, hi = _unpack_bf16_pair(_load_token_chunk(buf, (slot, k), c, tm, n_chunks))
            f_lo = f_lo + gate[k] * lo
            f_hi = f_hi + gate[k] * hi
        f_scr[:, c * LANES:(c + 1) * LANES] = f_lo
        f_scr[:, half + c * LANES:half + (c + 1) * LANES] = f_hi
    o_ref[...] = _layer_norm(alpha * x1_ref[...] + g2_ref[...] * f_scr[...], lng_ref[...], lnb_ref[...])


def _combine(rows, dest_flat, meta, x1, g2, ln_g, ln_b, y_slots, alpha):
    d = x1.shape[1]
    tm = rows.tm
    n_chunks = d // 2 // LANES
    last = rows.n_tiles - 1
    vec_spec = pl.BlockSpec((1, d), lambda i: (0, 0))
    return pl.pallas_call(
        functools.partial(_combine_kernel, alpha=alpha),
        grid=(rows.n_tiles,),
        in_specs=[
            pl.BlockSpec((tm * TOP_K,), lambda i: (i,), memory_space=pltpu.SMEM),
            pl.BlockSpec((tm * TOP_K,), lambda i: (jnp.minimum(i + 1, last),), memory_space=pltpu.SMEM),
            pl.BlockSpec((tm, LANES), lambda i: (i, 0)),
            pl.BlockSpec((tm, d), lambda i: (i, 0)),
            pl.BlockSpec((None, 1, d), lambda i: (rows.mod_row(i), 0, 0)),
            vec_spec, vec_spec,
            pl.BlockSpec(memory_space=pl.ANY),
        ],
        out_specs=pl.BlockSpec((tm, d), lambda i: (i, 0)),
        out_shape=jax.ShapeDtypeStruct((rows.n_rows, d), F32),
        scratch_shapes=[pltpu.VMEM((2, TOP_K, tm * n_chunks, LANES), I32), pltpu.VMEM((tm, d), F32),
                        pltpu.SemaphoreType.DMA((2,))],
        compiler_params=_params(("arbitrary",)),
        name="combine_ln",
    )(dest_flat, dest_flat, meta, x1, g2, ln_g.reshape(1, d), ln_b.reshape(1, d), y_slots)


def _slot_layout(meta, counts, n_rows):
    e = counts.shape[1]
    bm = EXPERT_ROWS
    cnt = counts[0].astype(I32)
    padded = (cnt + bm - 1) // bm * bm
    pend = jnp.cumsum(padded)
    pstart = pend - padded
    idx = meta[:, META_IDX:META_IDX + TOP_K].astype(I32)
    pos = meta[:, META_POS:META_POS + TOP_K].astype(I32)
    onehot = idx[..., None] == jnp.arange(e, dtype=I32)
    dest = jnp.sum(jnp.where(onehot, pstart, 0), axis=-1) + pos
    n_blocks = -(-(n_rows * TOP_K + e * (bm - 1)) // bm)
    n_used = pend[-1] // bm
    blk = jnp.minimum(jnp.arange(n_blocks, dtype=I32), n_used - 1) * bm
    block_e = jnp.minimum(jnp.sum(pend[None, :] <= blk[:, None], axis=-1), e - 1).astype(I32)
    pad_lo = jnp.concatenate([pstart + cnt, pend[-1:]]).astype(I32)
    pad_hi = jnp.concatenate([pend, jnp.full((1,), n_blocks * bm, I32)]).astype(I32)
    return dest.reshape(-1), block_e, n_used.reshape(1).astype(I32), pad_lo, pad_hi, n_blocks * bm


def _rope_tables(seq, pad_rows):
    rows = seq // GRID_W
    row = jnp.repeat(jnp.arange(rows, dtype=I32), GRID_W).astype(F32)
    col = jnp.tile(jnp.arange(GRID_W, dtype=I32), rows).astype(F32)
    half = HEAD_DIM // 2
    inv = ROPE_THETA ** (-jnp.arange(0, half, 2, dtype=F32) / half)
    ang = jnp.concatenate([row[:, None] * inv, col[:, None] * inv], axis=-1)
    cos = jnp.repeat(jnp.cos(ang), 2, axis=-1)
    sin = jnp.repeat(jnp.sin(ang), 2, axis=-1) * jnp.tile(jnp.array([-1.0, 1.0], F32), half)
    cos = jnp.concatenate([cos, jnp.ones((pad_rows, HEAD_DIM), F32)], axis=0)
    sin = jnp.concatenate([sin, jnp.zeros((pad_rows, HEAD_DIM), F32)], axis=0)
    return cos, sin


def kernel(x, c, ctx, c_ctx, ada_w, ada_b, w_in, q_norm_g, k_norm_g, conv_w, w_attn_o, w_conv_o, w_mix_o,
           ln1_g, ln1_b, router_w, router_b, w_up, b_up, w_down, b_down, ln2_g, ln2_b):
    batch, seq, d = x.shape
    ctx_len = ctx.shape[1]
    depth = ada_w.shape[0]
    n_q = d // HEAD_DIM
    d_q, d_kv, d_conv = d, n_q // GQA_GROUP * HEAD_DIM, d
    o_k, o_v = d_q, d_q + d_kv
    o_cb = o_v + d_kv
    o_cc, o_cx = o_cb + d_conv, o_cb + 2 * d_conv
    o_ga = o_cx + d_conv
    n_lat, n_ctx = batch * seq, batch * ctx_len
    n_tok = n_lat + n_ctx
    alpha = (2 * depth) ** 0.25
    assert seq & (seq - 1) == 0 and ctx_len & (ctx_len - 1) == 0 and seq % ctx_len == 0 and seq % GRID_W == 0

    tn = next(t for t in (512, 256, 128) if all(o % t == 0 for o in (o_k, o_v, o_cb, o_cc, o_cx, o_ga)))
    row_unit = math.gcd(seq, n_ctx)
    tm_proj, tm_merge, tm_small = _tile(row_unit, 1024), _tile(row_unit, 1024), _tile(row_unit, 256)

    mod_rows = -(-(batch + 1) // 8) * 8
    cvec = jnp.zeros((mod_rows, d), F32).at[:batch].set(c).at[batch].set(c_ctx)
    mod = _adaln(cvec, ada_w, ada_b).reshape(depth, mod_rows, 6, 1, d)

    cos, sin = _rope_tables(seq, tm_proj)
    scale = HEAD_DIM ** -0.5
    state = jnp.concatenate([x.reshape(n_lat, d), ctx.reshape(n_ctx, d)], axis=0)
    n_chunks = d // 2 // LANES
    w_in_b, w_attn_b, w_conv_b, w_mix_b = (w.astype(BF16) for w in (w_in, w_attn_o, w_conv_o, w_mix_o))
    pair = jnp.arange(HEAD_DIM) ^ 1
    n_vt, n_vcb = d_kv // tn, (d_kv + d_conv) // tn

    for l in range(depth):
        last = l == depth - 1
        sh1, sc1, g1, sh2, sc2, g2 = [mod[l, :, m] for m in range(6)]
        gq, gk = q_norm_g[l] * scale, k_norm_g[l]
        tab_a = jnp.stack([cos * gq, cos * gk])
        tab_b = jnp.stack([sin * gq[pair], sin * gk[pair]])
        n_main = n_lat if last else n_tok

        rows_main = _Rows(n_main, tm_proj, batch, seq, n_lat)
        qk, vcb, z, gates, w_up_b, w_down_b = _proj_fused(rows_main, state, sc1, sh1, w_in_b, l, tab_a, tab_b,
                                                          d_q, d_kv, d_conv, tn, w_up, w_down)
        if last:
            rows_ctx = _Rows(n_ctx, tm_proj, batch, seq, n_lat, tile0=n_lat // tm_proj)
            qk_ctx = _proj("qk", rows_ctx, state, sc1, sh1, w_in_b, l, 0, d_q + d_kv, tn,
                           extra=(tab_a, tab_b, d_q // tn), name="proj_qk_ctx")
            vcb_ctx = _proj("plain", rows_ctx, state, sc1, sh1, w_in_b, l, o_v, d_kv + d_conv, tn,
                            out_map=lambda i, j: (i, (j + n_vcb - n_vt) % n_vcb), name="proj_vcb_ctx")
            ctx_blk0 = 0
        else:
            qk_ctx, vcb_ctx, ctx_blk0 = qk, vcb, n_lat // ctx_len

        attn = _attention(qk, vcb, qk_ctx, vcb_ctx, ctx_blk0, batch, seq, ctx_len, d_q, d_conv, ctx_queries=not last)

        rows_m = _Rows(n_main, tm_merge, batch, seq, n_lat)
        y = _merge(rows_m, attn, vcb, z, gates, conv_w, w_attn_b, w_conv_b, l, seq, ctx_len)

        rows_s = _Rows(n_main, tm_small, batch, seq, n_lat)
        x1, vp, meta, counts = _mix_router(rows_s, y, w_mix_b, l, state, g1, ln1_g[l], ln1_b[l],
                                           sc2, sh2, router_w[l], router_b[l], alpha)
        dest, block_e, n_used, pad_lo, pad_hi, n_slots = _slot_layout(meta, counts, n_main)
        xg = _dispatch(vp, dest, pad_lo, pad_hi, n_slots, tm_small, n_chunks)
        y_slots = _experts(xg, block_e, n_used, w_up_b, b_up, w_down_b, b_down, l)
        state = _combine(rows_s, dest, meta, x1, g2, ln2_g[l], ln2_b[l], y_slots, alpha)

    return state[:n_lat].reshape(batch, seq, d)
```

```python
import functools
import math

import jax
import jax.numpy as jnp
from jax import lax
from jax.experimental import pallas as pl
from jax.experimental.pallas import tpu as pltpu

F32 = jnp.float32
BF16 = jnp.bfloat16
I32 = jnp.int32

LANES = 128
HEAD_DIM = 128
GQA_GROUP = 4
GRID_W = 64
TOP_K = 4
SWIGLU_ALPHA = 1.702
SWIGLU_LIMIT = 7.0
ROPE_THETA = 10000.0
NORM_EPS = 1e-6
VMEM_LIMIT = 56 * 1024 * 1024
EXPERT_ROWS = 256
META_IDX, META_GATE, META_POS = 0, 4, 8
DMA_UNROLL = 8
ATTN_Q_ROWS = 1024


def _params(sem, vmem=VMEM_LIMIT):
    return pltpu.CompilerParams(dimension_semantics=sem, vmem_limit_bytes=vmem)


def _tile(n, pref):
    if n <= pref:
        return n
    t = pref - pref % 8
    while n % t:
        t -= 8
    return t


def _adaln_kernel(c_ref, w_ref, b_ref, o_ref):
    c = c_ref[...]
    s = (c * jax.nn.sigmoid(c)).astype(BF16)
    o_ref[...] = jnp.dot(s, w_ref[...].astype(BF16), preferred_element_type=F32) + b_ref[...]


def _adaln(cvec, ada_w, ada_b):
    depth, d, n = ada_w.shape
    r = cvec.shape[0]
    tn = _tile(n, 1024)
    return pl.pallas_call(
        _adaln_kernel,
        grid=(depth, n // tn),
        in_specs=[
            pl.BlockSpec((r, d), lambda l, j: (0, 0)),
            pl.BlockSpec((None, d, tn), lambda l, j: (l, 0, j)),
            pl.BlockSpec((None, 1, tn), lambda l, j: (l, 0, j)),
        ],
        out_specs=pl.BlockSpec((None, r, tn), lambda l, j: (l, 0, j)),
        out_shape=jax.ShapeDtypeStruct((depth, r, n), F32),
        compiler_params=_params(("parallel", "parallel")),
        name="adaln",
    )(cvec, ada_w, ada_b.reshape(depth, 1, n))


def _modulate(u_scr, x_ref, sc_ref, sh_ref):
    @pl.when(pl.program_id(1) == 0)
    def _():
        u_scr[...] = (x_ref[...] * (1.0 + sc_ref[...]) + sh_ref[...]).astype(BF16)


def _sigmoid(x):
    return 0.5 * jnp.tanh(0.5 * x) + 0.5


def _norm_rope_store(acc, ta, tb, o_ref):
    tn = acc.shape[1]
    rr = lax.broadcasted_iota(I32, (HEAD_DIM, HEAD_DIM), 0)
    cc = lax.broadcasted_iota(I32, (HEAD_DIM, HEAD_DIM), 1)
    swap = jnp.where((rr ^ 1) == cc, 1.0, 0.0).astype(BF16)
    mean_w = jnp.full((HEAD_DIM, HEAD_DIM), 1.0 / HEAD_DIM, BF16)
    for h in range(tn // HEAD_DIM):
        cols = slice(h * HEAD_DIM, (h + 1) * HEAD_DIM)
        a = acc[:, cols]
        ms = jnp.dot((a * a).astype(BF16), mean_w, preferred_element_type=F32)
        partner = jnp.dot(a.astype(BF16), swap, preferred_element_type=F32)
        o_ref[:, cols] = (lax.rsqrt(ms + NORM_EPS) * (a * ta + partner * tb)).astype(BF16)


def _proj_qk_kernel(x_ref, sc_ref, sh_ref, w_ref, ta_ref, tb_ref, o_ref, u_scr):
    _modulate(u_scr, x_ref, sc_ref, sh_ref)
    acc = jnp.dot(u_scr[...], w_ref[...], preferred_element_type=F32)
    _norm_rope_store(acc, ta_ref[...], tb_ref[...], o_ref)


def _proj_plain_kernel(x_ref, sc_ref, sh_ref, w_ref, o_ref, u_scr):
    _modulate(u_scr, x_ref, sc_ref, sh_ref)
    o_ref[...] = jnp.dot(u_scr[...], w_ref[...], preferred_element_type=F32).astype(BF16)


def _proj_fused_kernel(x_ref, sc_ref, sh_ref, w_ref, wx_ref, ta_ref, tb_ref, wu_ref, wd_ref,
                       qk_ref, vcb_ref, z_ref, g_ref, wub_ref, wdb_ref, u_scr, *, n_qk, n_vcb, n_z, cast_steps):
    _modulate(u_scr, x_ref, sc_ref, sh_ref)
    j = pl.program_id(1)
    z0 = n_qk + n_vcb

    @pl.when(pl.program_id(0) * pl.num_programs(1) + j < cast_steps)
    def _():
        wub_ref[...] = wu_ref[...].astype(BF16)
        wdb_ref[...] = wd_ref[...].astype(BF16)

    @pl.when(j < n_qk)
    def _():
        acc = jnp.dot(u_scr[...], w_ref[...], preferred_element_type=F32)
        _norm_rope_store(acc, ta_ref[...], tb_ref[...], qk_ref)

    @pl.when((j >= n_qk) & (j < z0))
    def _():
        vcb_ref[...] = jnp.dot(u_scr[...], w_ref[...], preferred_element_type=F32).astype(BF16)

    @pl.when((j >= z0) & (j < z0 + n_z))
    def _():
        u = u_scr[...]
        c = jnp.dot(u, w_ref[...], preferred_element_type=F32)
        xin = jnp.dot(u, wx_ref[...], preferred_element_type=F32)
        z_ref[...] = (c * xin).astype(BF16)

    @pl.when(j >= z0 + n_z)
    def _():
        acc = jnp.dot(u_scr[...], w_ref[...], preferred_element_type=F32)
        g_ref[...] = _sigmoid(acc).astype(BF16)


class _Rows:
    def __init__(self, n_rows, tm, batch, seq, n_lat, tile0=0):
        assert n_rows % tm == 0 and seq % tm == 0 and n_lat % tm == 0
        self.n_rows, self.tm, self.batch, self.tile0 = n_rows, tm, batch, tile0
        self.n_tiles = n_rows // tm
        self.lat_tiles = n_lat // tm
        self.per_seq = seq // tm

    def state_tile(self, i):
        return i + self.tile0

    def mod_row(self, i):
        t = i + self.tile0
        return jnp.where(t < self.lat_tiles, t // self.per_seq, self.batch)

    def rope_blk(self, i):
        t = i + self.tile0
        return jnp.where(t < self.lat_tiles, t % self.per_seq, self.per_seq)


def _proj_common_specs(rows, d):
    return [
        pl.BlockSpec((rows.tm, d), lambda i, j: (rows.state_tile(i), 0)),
        pl.BlockSpec((None, 1, d), lambda i, j: (rows.mod_row(i), 0, 0)),
        pl.BlockSpec((None, 1, d), lambda i, j: (rows.mod_row(i), 0, 0)),
    ]


def _proj(kind, rows, x, sc, sh, w, layer, col_off, n_cols, tn, extra=(), out_map=None, name="proj"):
    d = x.shape[1]
    tm = rows.tm
    assert col_off % tn == 0
    off = col_off // tn
    in_specs = _proj_common_specs(rows, d) + [pl.BlockSpec((None, d, tn), lambda i, j: (layer, 0, off + j))]
    operands = [x, sc, sh, w]
    if kind == "qk":
        tab_a, tab_b, q_tiles = extra
        tab_spec = pl.BlockSpec((None, tm, HEAD_DIM), lambda i, j: (jnp.where(j < q_tiles, 0, 1), rows.rope_blk(i), 0))
        in_specs += [tab_spec, tab_spec]
        operands += [tab_a, tab_b]
    if out_map is None:
        out_map = lambda i, j: (i, j)
    return pl.pallas_call(
        {"qk": _proj_qk_kernel, "plain": _proj_plain_kernel}[kind],
        grid=(rows.n_tiles, n_cols // tn),
        in_specs=in_specs,
        out_specs=pl.BlockSpec((tm, tn), out_map),
        out_shape=jax.ShapeDtypeStruct((rows.n_rows, n_cols), BF16),
        scratch_shapes=[pltpu.VMEM((tm, d), BF16)],
        compiler_params=_params(("parallel", "arbitrary")),
        name=name,
    )(*operands)


def _proj_fused(rows, x, sc, sh, w, layer, tab_a, tab_b, d_q, d_kv, d_conv, tn, w_up, w_down):
    d = x.shape[1]
    tm = rows.tm
    n_qk, n_vt, n_vcb, n_z, n_g = (d_q + d_kv) // tn, d_kv // tn, (d_kv + d_conv) // tn, d_conv // tn, 2 * d // tn
    q_tiles = d_q // tn
    z0 = n_qk + n_vcb
    n_j = n_qk + n_vcb + n_z + n_g
    depth, e, _, f2 = w_up.shape
    cast_steps = 1 << ((rows.n_tiles * n_j).bit_length() - 1)
    up_rows, dn_rows = e * d // cast_steps, e * (f2 // 2) // cast_steps
    assert up_rows % 16 == 0 and dn_rows % 16 == 0

    def clamp(v, n):
        return jnp.clip(v, 0, n - 1)

    def cast_blk(i, j):
        return jnp.minimum(i * n_j + j, cast_steps - 1)

    tab_spec = pl.BlockSpec((None, tm, HEAD_DIM), lambda i, j: (jnp.where(j < q_tiles, 0, 1), rows.rope_blk(i), 0))
    in_specs = _proj_common_specs(rows, d) + [
        pl.BlockSpec((None, d, tn), lambda i, j: (layer, 0, j + jnp.where(j >= z0 + n_z, n_z, 0))),
        pl.BlockSpec((None, d, tn), lambda i, j: (layer, 0, z0 + n_z + clamp(j - z0, n_z))),
        tab_spec, tab_spec,
        pl.BlockSpec((up_rows, f2), lambda i, j: (layer * cast_steps + cast_blk(i, j), 0)),
        pl.BlockSpec((dn_rows, d), lambda i, j: (layer * cast_steps + cast_blk(i, j), 0)),
    ]
    out_specs = [
        pl.BlockSpec((tm, tn), lambda i, j: (i, clamp(j, n_qk))),
        pl.BlockSpec((tm, tn), lambda i, j: (i, (clamp(j - n_qk, n_vcb) + n_vcb - n_vt) % n_vcb)),
        pl.BlockSpec((tm, tn), lambda i, j: (i, clamp(j - z0, n_z))),
        pl.BlockSpec((tm, tn), lambda i, j: (i, clamp(j - z0 - n_z, n_g))),
        pl.BlockSpec((up_rows, f2), lambda i, j: (cast_blk(i, j), 0)),
        pl.BlockSpec((dn_rows, d), lambda i, j: (cast_blk(i, j), 0)),
    ]
    out_shape = [jax.ShapeDtypeStruct((rows.n_rows, n * tn), BF16) for n in (n_qk, n_vcb, n_z, n_g)]
    out_shape += [jax.ShapeDtypeStruct((e * d, f2), BF16), jax.ShapeDtypeStruct((e * (f2 // 2), d), BF16)]
    qk, vcb, z, gates, w_up_b, w_down_b = pl.pallas_call(
        functools.partial(_proj_fused_kernel, n_qk=n_qk, n_vcb=n_vcb, n_z=n_z, cast_steps=cast_steps),
        grid=(rows.n_tiles, n_j),
        in_specs=in_specs,
        out_specs=out_specs,
        out_shape=out_shape,
        scratch_shapes=[pltpu.VMEM((tm, d), BF16)],
        compiler_params=_params(("arbitrary", "arbitrary")),
        name="proj_fused",
    )(x, sc, sh, w, w, tab_a, tab_b, w_up.reshape(depth * e * d, f2), w_down.reshape(depth * e * (f2 // 2), d))
    return qk, vcb, z, gates, w_up_b.reshape(1, e, d, f2), w_down_b.reshape(1, e, f2 // 2, d)


def _attn_heads(q_ref, kl_ref, kc_ref, vl_ext, vc_ext, o_ref, with_latent):
    nt = (((1,), (1,)), ((), ()))
    unit = kc_ref.shape[0]
    for r0 in range(0, q_ref.shape[0], unit):
        rows = slice(r0, r0 + unit)
        for h in range(GQA_GROUP):
            cols = slice(h * HEAD_DIM, (h + 1) * HEAD_DIM)
            q = q_ref[rows, cols]
            s_c = lax.dot_general(q, kc_ref[...], nt, preferred_element_type=F32)
            m = jnp.max(s_c, axis=-1, keepdims=True)
            if with_latent:
                s_l = lax.dot_general(q, kl_ref[...], nt, preferred_element_type=F32)
                m = jnp.maximum(m, jnp.max(s_l, axis=-1, keepdims=True))
            p_c = jnp.exp((s_c - m).astype(BF16))
            o = jnp.dot(p_c, vc_ext[...], preferred_element_type=F32)
            if with_latent:
                p_l = jnp.exp((s_l - m).astype(BF16))
                o = o + jnp.dot(p_l, vl_ext[...], preferred_element_type=F32)
            o_ref[rows, cols] = (o[:, :HEAD_DIM] / o[:, HEAD_DIM:]).astype(BF16)


def _attn_kernel(q_ref, kl_ref, vl_ref, kc_ref, vc_ref, o_ref, vl_ext, vc_ext, *, lat_steps, ctx_step):
    qi = pl.program_id(2)

    @pl.when(qi == 0)
    def _():
        vl_ext[:, :HEAD_DIM] = vl_ref[...]
        vl_ext[:, HEAD_DIM:] = jnp.ones_like(vl_ref)
        vc_ext[:, :HEAD_DIM] = vc_ref[...]
        vc_ext[:, HEAD_DIM:] = jnp.ones_like(vc_ref)

    if not ctx_step:
        _attn_heads(q_ref, kl_ref, kc_ref, vl_ext, vc_ext, o_ref, True)
        return

    @pl.when(qi < lat_steps)
    def _():
        _attn_heads(q_ref, kl_ref, kc_ref, vl_ext, vc_ext, o_ref, True)

    @pl.when(qi >= lat_steps)
    def _():
        _attn_heads(q_ref, kl_ref, kc_ref, vl_ext, vc_ext, o_ref, False)


def _attn_ctx_kernel(q_ref, kc_ref, vc_ref, o_ref, vc_ext):
    vc_ext[:, :HEAD_DIM] = vc_ref[...]
    vc_ext[:, HEAD_DIM:] = jnp.ones_like(vc_ref)
    _attn_heads(q_ref, None, kc_ref, None, vc_ext, o_ref, False)


def _attention_ctx(qk, vcb, ctx_blk0, batch, ctx_len, d_q, d_conv):
    n_kv = d_q // HEAD_DIM // GQA_GROUP
    gw = GQA_GROUP * HEAD_DIM
    k_col = d_q // HEAD_DIM
    v_col = d_conv // HEAD_DIM
    return pl.pallas_call(
        _attn_ctx_kernel,
        grid=(batch, n_kv),
        in_specs=[
            pl.BlockSpec((ctx_len, gw), lambda b, g: (ctx_blk0 + b, g)),
            pl.BlockSpec((ctx_len, HEAD_DIM), lambda b, g: (ctx_blk0 + b, k_col + g)),
            pl.BlockSpec((ctx_len, HEAD_DIM), lambda b, g: (ctx_blk0 + b, v_col + g)),
        ],
        out_specs=pl.BlockSpec((ctx_len, gw), lambda b, g: (b, g)),
        out_shape=jax.ShapeDtypeStruct((batch * ctx_len, d_q), BF16),
        scratch_shapes=[pltpu.VMEM((ctx_len, 2 * HEAD_DIM), BF16)],
        compiler_params=_params(("parallel", "parallel")),
        name="attention_ctx",
    )(qk, qk, vcb)


def _attention(qk, vcb, qk_ctx, vcb_ctx, ctx_blk0, batch, seq, ctx_len, d_q, d_conv, ctx_queries):
    n_kv = d_q // HEAD_DIM // GQA_GROUP
    tq = ctx_len if ctx_queries else _tile(seq, ATTN_Q_ROWS)
    lat_steps = seq // tq
    gw = GQA_GROUP * HEAD_DIM
    k_col = d_q // HEAD_DIM
    v_col = d_conv // HEAD_DIM
    n_rows = batch * seq + (batch * ctx_len if ctx_queries else 0)

    def q_map(b, g, qi):
        return (jnp.where(qi < lat_steps, b * lat_steps + qi, batch * lat_steps + b), g)

    return pl.pallas_call(
        functools.partial(_attn_kernel, lat_steps=lat_steps, ctx_step=ctx_queries),
        grid=(batch, n_kv, lat_steps + (1 if ctx_queries else 0)),
        in_specs=[
            pl.BlockSpec((tq, gw), q_map),
            pl.BlockSpec((seq, HEAD_DIM), lambda b, g, qi: (b, k_col + g)),
            pl.BlockSpec((seq, HEAD_DIM), lambda b, g, qi: (b, v_col + g)),
            pl.BlockSpec((ctx_len, HEAD_DIM), lambda b, g, qi: (ctx_blk0 + b, k_col + g)),
            pl.BlockSpec((ctx_len, HEAD_DIM), lambda b, g, qi: (ctx_blk0 + b, v_col + g)),
        ],
        out_specs=pl.BlockSpec((tq, gw), q_map),
        out_shape=jax.ShapeDtypeStruct((n_rows, d_q), BF16),
        scratch_shapes=[pltpu.VMEM((seq, 2 * HEAD_DIM), BF16), pltpu.VMEM((ctx_len, 2 * HEAD_DIM), BF16)],
        compiler_params=_params(("parallel", "parallel", "arbitrary")),
        name="attention",
    )(qk, qk, vcb, qk_ctx, vcb_ctx)


def _merge_kernel(attn_ref, attnc_ref, cb_ref, z_ref, zprev_ref, znext_ref, cw_ref, wa_ref, wc_ref, ga_ref, gc_ref,
                  o_ref, conv_scr, *, rows, seq, ctx_len, chunk):
    i = pl.program_id(0)

    @pl.when(pl.program_id(1) == 0)
    def _():
        tm, d = conv_scr.shape
        seq_len = jnp.where(i < rows.lat_tiles, seq, ctx_len)
        r = lax.broadcasted_iota(I32, (tm, chunk), 0)
        pos = (r + i * tm) & (seq_len - 1)
        first, last = pos == 0, pos == seq_len - 1
        for c0 in range(0, d, chunk):
            cols = slice(c0, c0 + chunk)
            z = z_ref[:, cols].astype(F32)
            z_prev = zprev_ref[:, cols].astype(F32)
            z_next = znext_ref[:, cols].astype(F32)
            before = jnp.where(r == 0, z_prev[7:8], pltpu.roll(z, 1, 0))
            after = jnp.where(r == tm - 1, z_next[0:1], pltpu.roll(z, tm - 1, 0))
            before = jnp.where(first, 0.0, before)
            after = jnp.where(last, 0.0, after)
            conv = cw_ref[0:1, cols] * before + cw_ref[1:2, cols] * z + cw_ref[2:3, cols] * after
            conv_scr[:, cols] = (cb_ref[:, cols].astype(F32) * conv).astype(BF16)

    def finish(a_ref):
        a = jnp.dot(a_ref[...], wa_ref[...], preferred_element_type=F32)
        c = jnp.dot(conv_scr[...], wc_ref[...], preferred_element_type=F32)
        o_ref[...] = (ga_ref[...].astype(F32) * a + gc_ref[...].astype(F32) * c).astype(BF16)

    @pl.when(i < rows.lat_tiles)
    def _():
        finish(attn_ref)

    @pl.when(i >= rows.lat_tiles)
    def _():
        finish(attnc_ref)


def _merge(rows, attn, attn_ctx, vcb, z, gates, conv_w, w_attn_o, w_conv_o, layer, seq, ctx_len):
    d = attn.shape[1]
    tm = rows.tm
    tn = _tile(d, 512)
    n_j = d // tn
    halo = tm // 8
    last_halo = rows.n_rows // 8 - 1
    lat_last = attn.shape[0] // tm - 1
    ctx_last = attn_ctx.shape[0] // tm - 1
    return pl.pallas_call(
        functools.partial(_merge_kernel, rows=rows, seq=seq, ctx_len=ctx_len, chunk=_tile(d, 512)),
        grid=(rows.n_tiles, n_j),
        in_specs=[
            pl.BlockSpec((tm, d), lambda i, j: (jnp.minimum(i, lat_last), 0)),
            pl.BlockSpec((tm, d), lambda i, j: (jnp.clip(i - rows.lat_tiles, 0, ctx_last), 0)),
            pl.BlockSpec((tm, d), lambda i, j: (i, 0)),
            pl.BlockSpec((tm, d), lambda i, j: (i, 0)),
            pl.BlockSpec((8, d), lambda i, j: (jnp.maximum(i * halo - 1, 0), 0)),
            pl.BlockSpec((8, d), lambda i, j: (jnp.minimum((i + 1) * halo, last_halo), 0)),
            pl.BlockSpec((None, 3, d), lambda i, j: (layer, 0, 0)),
            pl.BlockSpec((None, d, tn), lambda i, j: (layer, 0, j)),
            pl.BlockSpec((None, d, tn), lambda i, j: (layer, 0, j)),
            pl.BlockSpec((tm, tn), lambda i, j: (i, j)),
            pl.BlockSpec((tm, tn), lambda i, j: (i, n_j + j)),
        ],
        out_specs=pl.BlockSpec((tm, tn), lambda i, j: (i, j)),
        out_shape=jax.ShapeDtypeStruct((rows.n_rows, d), BF16),
        scratch_shapes=[pltpu.VMEM((tm, d), BF16)],
        compiler_params=_params(("parallel", "arbitrary")),
        name="merge",
    )(attn, attn_ctx, vcb, z, z, z, conv_w, w_attn_o, w_conv_o, gates, gates)


def _layer_norm(r, g, b):
    mu = jnp.mean(r, axis=-1, keepdims=True)
    c = r - mu
    var = jnp.mean(c * c, axis=-1, keepdims=True)
    return c * lax.rsqrt(var + NORM_EPS) * g + b


def _pack_bf16_pair(lo, hi):
    lo_bits = lax.bitcast_convert_type(lo.astype(BF16).astype(F32), I32)
    hi_bits = lax.bitcast_convert_type(hi.astype(BF16).astype(F32), I32)
    return lax.shift_right_logical(lo_bits, jnp.full_like(lo_bits, 16)) | (hi_bits & jnp.int32(-65536))


def _unpack_bf16_pair(w):
    lo = lax.bitcast_convert_type(lax.shift_left(w, jnp.full_like(w, 16)), F32)
    hi = lax.bitcast_convert_type(w & jnp.int32(-65536), F32)
    return lo, hi


def _store_token_rows(ref, packed):
    tm, half = packed.shape
    n_chunks = half // LANES
    for c in range(n_chunks):
        ref[pl.ds(c, tm, stride=n_chunks), :] = packed[:, c * LANES:(c + 1) * LANES]


def _load_token_chunk(ref, lead, c, tm, n_chunks):
    return ref[(*lead, pl.ds(c, tm, stride=n_chunks), slice(None))]


def _mix_router_kernel(y_ref, wmix_ref, x_ref, g1_ref, lng_ref, lnb_ref, sc2_ref, sh2_ref, rw2_ref, rwh_ref, rb_ref,
                       x1_ref, vp_ref, meta_ref, cnt_ref, run_scr, *, alpha, n_experts):
    @pl.when(pl.program_id(0) == 0)
    def _():
        run_scr[...] = jnp.zeros_like(run_scr)

    mix = jnp.dot(y_ref[...], wmix_ref[...], preferred_element_type=F32)
    x1 = _layer_norm(alpha * x_ref[...] + g1_ref[...] * mix, lng_ref[...], lnb_ref[...])
    x1_ref[...] = x1
    v = x1 * (1.0 + sc2_ref[...]) + sh2_ref[...]
    tm, d = v.shape
    _store_token_rows(vp_ref, _pack_bf16_pair(v[:, : d // 2], v[:, d // 2:]))

    v_hi = v.astype(BF16)
    v_lo = (v - v_hi.astype(F32)).astype(BF16)
    hh_hl = jnp.dot(v_hi, rw2_ref[...], preferred_element_type=F32)
    lh = jnp.dot(v_lo, rwh_ref[...], preferred_element_type=F32)
    logits = hh_hl[:, :n_experts] + hh_hl[:, n_experts:] + lh + rb_ref[...]
    lane = lax.broadcasted_iota(I32, (tm, n_experts), 1).astype(F32)
    work = logits
    sel, val = [], []
    for _ in range(TOP_K):
        m = jnp.max(work, axis=-1, keepdims=True)
        s = jnp.min(jnp.where(work == m, lane, float(n_experts)), axis=-1, keepdims=True)
        sel.append(s)
        val.append(m)
        work = jnp.where(lane == s, -jnp.inf, work)
    ex = [jnp.exp(v_k - val[0]) for v_k in val]
    den = ex[0] + ex[1] + ex[2] + ex[3]

    onehot = jnp.zeros((tm, n_experts), F32)
    for s in sel:
        onehot = onehot + jnp.where(lane == s, 1.0, 0.0)
    rr = lax.broadcasted_iota(I32, (tm, tm), 0)
    cc = lax.broadcasted_iota(I32, (tm, tm), 1)
    tri = jnp.where(cc <= rr, 1.0, 0.0).astype(BF16)
    incl = jnp.dot(tri, onehot.astype(BF16), preferred_element_type=F32)
    before = run_scr[...] + incl - onehot

    mlane = lax.broadcasted_iota(I32, (tm, LANES), 1)
    meta = jnp.zeros((tm, LANES), F32)
    for k in range(TOP_K):
        pos_k = jnp.sum(jnp.where(lane == sel[k], before, 0.0), axis=-1, keepdims=True)
        meta = jnp.where(mlane == META_IDX + k, sel[k], meta)
        meta = jnp.where(mlane == META_GATE + k, ex[k] / den, meta)
        meta = jnp.where(mlane == META_POS + k, pos_k, meta)
    meta_ref[...] = meta
    run_scr[...] = run_scr[...] + incl[tm - 1:tm, :]
    cnt_ref[...] = run_scr[...]


def _mix_router(rows, y, w_mix, layer, x, g1, ln_g, ln_b, sc2, sh2, router_w, router_b, alpha):
    d = x.shape[1]
    tm = rows.tm
    e = router_w.shape[1]
    n_chunks = d // 2 // LANES
    rw_hi = router_w.astype(BF16)
    rw_lo = (router_w - rw_hi.astype(F32)).astype(BF16)
    mod_spec = pl.BlockSpec((None, 1, d), lambda i: (rows.mod_row(i), 0, 0))
    vec_spec = pl.BlockSpec((1, d), lambda i: (0, 0))
    return pl.pallas_call(
        functools.partial(_mix_router_kernel, alpha=alpha, n_experts=e),
        grid=(rows.n_tiles,),
        in_specs=[
            pl.BlockSpec((tm, d), lambda i: (i, 0)),
            pl.BlockSpec((None, d, d), lambda i: (layer, 0, 0)),
            pl.BlockSpec((tm, d), lambda i: (i, 0)),
            mod_spec, vec_spec, vec_spec, mod_spec, mod_spec,
            pl.BlockSpec((d, 2 * e), lambda i: (0, 0)),
            pl.BlockSpec((d, e), lambda i: (0, 0)),
            pl.BlockSpec((1, e), lambda i: (0, 0)),
        ],
        out_specs=[
            pl.BlockSpec((tm, d), lambda i: (i, 0)),
            pl.BlockSpec((tm * n_chunks, LANES), lambda i: (i, 0)),
            pl.BlockSpec((tm, LANES), lambda i: (i, 0)),
            pl.BlockSpec((1, e), lambda i: (0, 0)),
        ],
        out_shape=[
            jax.ShapeDtypeStruct((rows.n_rows, d), F32),
            jax.ShapeDtypeStruct((rows.n_rows * n_chunks, LANES), I32),
            jax.ShapeDtypeStruct((rows.n_rows, LANES), F32),
            jax.ShapeDtypeStruct((1, e), F32),
        ],
        scratch_shapes=[pltpu.VMEM((1, e), F32)],
        compiler_params=_params(("arbitrary",)),
        name="mix_ln_router",
    )(y, w_mix, x, g1, ln_g.reshape(1, d), ln_b.reshape(1, d), sc2, sh2,
      jnp.concatenate([rw_hi, rw_lo], axis=1), rw_hi, router_b.reshape(1, e))


def _dispatch_kernel(pad_lo_ref, pad_hi_ref, dest_ref, vp_ref, xg_ref, zero_scr, sem, zero_sem, *, tm, n_chunks):
    def slot_rows(ref, s):
        return ref.at[pl.ds(pl.multiple_of(s * n_chunks, n_chunks), n_chunks), :]

    @pl.when(pl.program_id(0) == 0)
    def _():
        zero_scr[...] = jnp.zeros_like(zero_scr)

        def zero_copy(s):
            return pltpu.make_async_copy(zero_scr, slot_rows(xg_ref, s), zero_sem)

        def start(s, carry):
            zero_copy(s).start()
            return carry

        def wait(s, carry):
            zero_copy(s).wait()
            return carry

        for e in range(pad_lo_ref.shape[0]):
            lax.fori_loop(pad_lo_ref[e], pad_hi_ref[e], start, 0)
        for e in range(pad_lo_ref.shape[0]):
            lax.fori_loop(pad_lo_ref[e], pad_hi_ref[e], wait, 0)

    def row_copy(r, d):
        return pltpu.make_async_copy(slot_rows(vp_ref, r), slot_rows(xg_ref, d), sem)

    def issue(r, carry):
        for k in range(TOP_K):
            row_copy(r, dest_ref[r * TOP_K + k]).start(priority=k % 2)
        return carry

    lax.fori_loop(0, tm, issue, 0, unroll=DMA_UNROLL)

    def drain(r, carry):
        row_copy(0, 0).wait()
        return carry

    lax.fori_loop(0, tm * TOP_K, drain, 0, unroll=DMA_UNROLL)


def _dispatch(vp, dest_flat, pad_lo, pad_hi, n_slots, tm, n_chunks):
    n_rows = vp.shape[0] // n_chunks
    grid_spec = pltpu.PrefetchScalarGridSpec(
        num_scalar_prefetch=2,
        grid=(n_rows // tm,),
        in_specs=[
            pl.BlockSpec((tm * TOP_K,), lambda i, lo, hi: (i,), memory_space=pltpu.SMEM),
            pl.BlockSpec((tm * n_chunks, LANES), lambda i, lo, hi: (i, 0)),
        ],
        out_specs=pl.BlockSpec(memory_space=pl.ANY),
        scratch_shapes=[pltpu.VMEM((n_chunks, LANES), I32), pltpu.SemaphoreType.DMA(()), pltpu.SemaphoreType.DMA(())],
    )
    return pl.pallas_call(
        functools.partial(_dispatch_kernel, tm=tm, n_chunks=n_chunks),
        grid_spec=grid_spec,
        out_shape=jax.ShapeDtypeStruct((n_slots * n_chunks, LANES), I32),
        compiler_params=_params(("arbitrary",)),
        name="dispatch",
    )(pad_lo, pad_hi, dest_flat, vp)


def _expert_kernel(be_ref, nused_ref, xg_ref, wup_ref, bup_ref, wdn_ref, bdn_ref, y_ref, x_scr):
    del be_ref
    i = pl.program_id(0)

    @pl.when(i < nused_ref[0])
    def _():
        bm, d = x_scr.shape
        half = d // 2
        n_chunks = half // LANES
        for c in range(n_chunks):
            lo, hi = _unpack_bf16_pair(_load_token_chunk(xg_ref, (), c, bm, n_chunks))
            x_scr[:, c * LANES:(c + 1) * LANES] = lo.astype(BF16)
            x_scr[:, half + c * LANES:half + (c + 1) * LANES] = hi.astype(BF16)
        h = jnp.dot(x_scr[...], wup_ref[...], preferred_element_type=F32) + bup_ref[...]
        f = h.shape[1] // 2
        glu = jnp.minimum(h[:, :f], SWIGLU_LIMIT)
        lin = jnp.clip(h[:, f:], -SWIGLU_LIMIT, SWIGLU_LIMIT)
        act = glu * _sigmoid(SWIGLU_ALPHA * glu) * (lin + 1.0)
        y = jnp.dot(act.astype(BF16), wdn_ref[...], preferred_element_type=F32) + bdn_ref[...]
        _store_token_rows(y_ref, _pack_bf16_pair(y[:, :half], y[:, half:]))

    @pl.when(i >= nused_ref[0])
    def _():
        y_ref[...] = jnp.zeros_like(y_ref)


def _experts(xg, block_e, n_used, w_up, b_up, w_down, b_down, layer):
    _, e, d, f2 = w_up.shape
    n_chunks = d // 2 // LANES
    bm = EXPERT_ROWS
    n_blocks = xg.shape[0] // n_chunks // bm

    def x_map(i, be, nu):
        return (jnp.maximum(jnp.minimum(i, nu[0] - 1), 0), 0)

    def w_map(i, be, nu):
        return (0, be[i], 0, 0)

    def b_map(i, be, nu):
        return (layer, be[i], 0, 0)

    grid_spec = pltpu.PrefetchScalarGridSpec(
        num_scalar_prefetch=2,
        grid=(n_blocks,),
        in_specs=[
            pl.BlockSpec((bm * n_chunks, LANES), x_map),
            pl.BlockSpec((None, None, d, f2), w_map),
            pl.BlockSpec((None, None, 1, f2), b_map),
            pl.BlockSpec((None, None, f2 // 2, d), w_map),
            pl.BlockSpec((None, None, 1, d), b_map),
        ],
        out_specs=pl.BlockSpec((bm * n_chunks, LANES), lambda i, be, nu: (i, 0)),
        scratch_shapes=[pltpu.VMEM((bm, d), BF16)],
    )
    depth = b_up.shape[0]
    return pl.pallas_call(
        _expert_kernel,
        grid_spec=grid_spec,
        out_shape=jax.ShapeDtypeStruct(xg.shape, I32),
        compiler_params=_params(("arbitrary",)),
        name="experts",
    )(block_e, n_used, xg, w_up, b_up.reshape(depth, e, 1, f2), w_down, b_down.reshape(depth, e, 1, d))


def _combine_kernel(dcur_ref, dnext_ref, meta_ref, x1_ref, g2_ref, lng_ref, lnb_ref, y_hbm,
                    o_ref, buf, f_scr, sem, *, alpha):
    i = pl.program_id(0)
    n = pl.num_programs(0)
    tm, d = x1_ref.shape
    half = d // 2
    n_chunks = half // LANES

    def row_copy(d_slot, slot, r, k):
        src = y_hbm.at[pl.ds(pl.multiple_of(d_slot * n_chunks, n_chunks), n_chunks), :]
        dst = buf.at[slot, k, pl.ds(pl.multiple_of(r * n_chunks, n_chunks), n_chunks), :]
        return pltpu.make_async_copy(src, dst, sem.at[slot])

    def issue_tile(dref, slot):
        def body(r, carry):
            for k in range(TOP_K):
                row_copy(dref[r * TOP_K + k], slot, r, k).start(priority=k % 2)
            return carry
        lax.fori_loop(0, tm, body, 0, unroll=DMA_UNROLL)

    @pl.when(i == 0)
    def _():
        issue_tile(dcur_ref, 0)

    @pl.when(i + 1 < n)
    def _():
        issue_tile(dnext_ref, (i + 1) % 2)

    slot = i % 2

    def drain(r, carry):
        row_copy(0, slot, 0, 0).wait()
        return carry

    lax.fori_loop(0, tm * TOP_K, drain, 0, unroll=DMA_UNROLL)

    meta = meta_ref[...]
    gate = [meta[:, META_GATE + k:META_GATE + k + 1] for k in range(TOP_K)]
    for c in range(n_chunks):
        f_lo = jnp.zeros((tm, LANES), F32)
        f_hi = jnp.zeros((tm, LANES), F32)
        for k in range(TOP_K):
            lo, hi = _unpack_bf16_pair(_load_token_chunk(buf, (slot, k), c, tm, n_chunks))
            f_lo = f_lo + gate[k] * lo
            f_hi = f_hi + gate[k] * hi
        f_scr[:, c * LANES:(c + 1) * LANES] = f_lo
        f_scr[:, half + c * LANES:half + (c + 1) * LANES] = f_hi
    o_ref[...] = _layer_norm(alpha * x1_ref[...] + g2_ref[...] * f_scr[...], lng_ref[...], lnb_ref[...])


def _combine(rows, dest_flat, meta, x1, g2, ln_g, ln_b, y_slots, alpha):
    d = x1.shape[1]
    tm = rows.tm
    n_chunks = d // 2 // LANES
    last = rows.n_tiles - 1
    vec_spec = pl.BlockSpec((1, d), lambda i: (0, 0))
    return pl.pallas_call(
        functools.partial(_combine_kernel, alpha=alpha),
        grid=(rows.n_tiles,),
        in_specs=[
            pl.BlockSpec((tm * TOP_K,), lambda i: (i,), memory_space=pltpu.SMEM),
            pl.BlockSpec((tm * TOP_K,), lambda i: (jnp.minimum(i + 1, last),), memory_space=pltpu.SMEM),
            pl.BlockSpec((tm, LANES), lambda i: (i, 0)),
            pl.BlockSpec((tm, d), lambda i: (i, 0)),
            pl.BlockSpec((None, 1, d), lambda i: (rows.mod_row(i), 0, 0)),
            vec_spec, vec_spec,
            pl.BlockSpec(memory_space=pl.ANY),
        ],
        out_specs=pl.BlockSpec((tm, d), lambda i: (i, 0)),
        out_shape=jax.ShapeDtypeStruct((rows.n_rows, d), F32),
        scratch_shapes=[pltpu.VMEM((2, TOP_K, tm * n_chunks, LANES), I32), pltpu.VMEM((tm, d), F32),
                        pltpu.SemaphoreType.DMA((2,))],
        compiler_params=_params(("arbitrary",)),
        name="combine_ln",
    )(dest_flat, dest_flat, meta, x1, g2, ln_g.reshape(1, d), ln_b.reshape(1, d), y_slots)


def _slot_layout(meta, counts, n_rows):
    e = counts.shape[1]
    bm = EXPERT_ROWS
    cnt = counts[0].astype(I32)
    padded = (cnt + bm - 1) // bm * bm
    pend = jnp.cumsum(padded)
    pstart = pend - padded
    idx = meta[:, META_IDX:META_IDX + TOP_K].astype(I32)
    pos = meta[:, META_POS:META_POS + TOP_K].astype(I32)
    onehot = idx[..., None] == jnp.arange(e, dtype=I32)
    dest = jnp.sum(jnp.where(onehot, pstart, 0), axis=-1) + pos
    n_blocks = -(-(n_rows * TOP_K + e * (bm - 1)) // bm)
    n_used = pend[-1] // bm
    blk = jnp.minimum(jnp.arange(n_blocks, dtype=I32), n_used - 1) * bm
    block_e = jnp.minimum(jnp.sum(pend[None, :] <= blk[:, None], axis=-1), e - 1).astype(I32)
    pad_lo = jnp.concatenate([pstart + cnt, pend[-1:]]).astype(I32)
    pad_hi = jnp.concatenate([pend, jnp.full((1,), n_blocks * bm, I32)]).astype(I32)
    return dest.reshape(-1), block_e, n_used.reshape(1).astype(I32), pad_lo, pad_hi, n_blocks * bm


def _rope_tables(seq, pad_rows):
    rows = seq // GRID_W
    row = jnp.repeat(jnp.arange(rows, dtype=I32), GRID_W).astype(F32)
    col = jnp.tile(jnp.arange(GRID_W, dtype=I32), rows).astype(F32)
    half = HEAD_DIM // 2
    inv = ROPE_THETA ** (-jnp.arange(0, half, 2, dtype=F32) / half)
    ang = jnp.concatenate([row[:, None] * inv, col[:, None] * inv], axis=-1)
    cos = jnp.repeat(jnp.cos(ang), 2, axis=-1)
    sin = jnp.repeat(jnp.sin(ang), 2, axis=-1) * jnp.tile(jnp.array([-1.0, 1.0], F32), half)
    cos = jnp.concatenate([cos, jnp.ones((pad_rows, HEAD_DIM), F32)], axis=0)
    sin = jnp.concatenate([sin, jnp.zeros((pad_rows, HEAD_DIM), F32)], axis=0)
    return cos, sin


def kernel(x, c, ctx, c_ctx, ada_w, ada_b, w_in, q_norm_g, k_norm_g, conv_w, w_attn_o, w_conv_o, w_mix_o,
           ln1_g, ln1_b, router_w, router_b, w_up, b_up, w_down, b_down, ln2_g, ln2_b):
    batch, seq, d = x.shape
    ctx_len = ctx.shape[1]
    depth = ada_w.shape[0]
    n_q = d // HEAD_DIM
    d_q, d_kv, d_conv = d, n_q // GQA_GROUP * HEAD_DIM, d
    o_k, o_v = d_q, d_q + d_kv
    o_cb = o_v + d_kv
    o_cc, o_cx = o_cb + d_conv, o_cb + 2 * d_conv
    o_ga = o_cx + d_conv
    n_lat, n_ctx = batch * seq, batch * ctx_len
    n_tok = n_lat + n_ctx
    alpha = (2 * depth) ** 0.25
    assert seq & (seq - 1) == 0 and ctx_len & (ctx_len - 1) == 0 and seq % ctx_len == 0 and seq % GRID_W == 0

    tn = next(t for t in (512, 256, 128) if all(o % t == 0 for o in (o_k, o_v, o_cb, o_cc, o_cx, o_ga)))
    row_unit = math.gcd(seq, n_ctx)
    tm_proj, tm_merge, tm_small = _tile(row_unit, 1024), _tile(row_unit, 1024), _tile(row_unit, 256)

    mod_rows = -(-(batch + 1) // 8) * 8
    cvec = jnp.zeros((mod_rows, d), F32).at[:batch].set(c).at[batch].set(c_ctx)
    mod = _adaln(cvec, ada_w, ada_b).reshape(depth, mod_rows, 6, 1, d)

    cos, sin = _rope_tables(seq, tm_proj)
    scale = HEAD_DIM ** -0.5
    state = jnp.concatenate([x.reshape(n_lat, d), ctx.reshape(n_ctx, d)], axis=0)
    n_chunks = d // 2 // LANES
    w_in_b, w_attn_b, w_conv_b, w_mix_b = (w.astype(BF16) for w in (w_in, w_attn_o, w_conv_o, w_mix_o))
    pair = jnp.arange(HEAD_DIM) ^ 1
    n_vt, n_vcb = d_kv // tn, (d_kv + d_conv) // tn

    for l in range(depth):
        last = l == depth - 1
        sh1, sc1, g1, sh2, sc2, g2 = [mod[l, :, m] for m in range(6)]
        gq, gk = q_norm_g[l] * scale, k_norm_g[l]
        tab_a = jnp.stack([cos * gq, cos * gk])
        tab_b = jnp.stack([sin * gq[pair], sin * gk[pair]])
        n_main = n_lat if last else n_tok

        rows_main = _Rows(n_main, tm_proj, batch, seq, n_lat)
        qk, vcb, z, gates, w_up_b, w_down_b = _proj_fused(rows_main, state, sc1, sh1, w_in_b, l, tab_a, tab_b,
                                                          d_q, d_kv, d_conv, tn, w_up, w_down)
        if last:
            rows_ctx = _Rows(n_ctx, tm_proj, batch, seq, n_lat, tile0=n_lat // tm_proj)
            qk_ctx = _proj("qk", rows_ctx, state, sc1, sh1, w_in_b, l, 0, d_q + d_kv, tn,
                           extra=(tab_a, tab_b, d_q // tn), name="proj_qk_ctx")
            vcb_ctx = _proj("plain", rows_ctx, state, sc1, sh1, w_in_b, l, o_v, d_kv + d_conv, tn,
                            out_map=lambda i, j: (i, (j + n_vcb - n_vt) % n_vcb), name="proj_vcb_ctx")
            ctx_blk0 = 0
        else:
            qk_ctx, vcb_ctx, ctx_blk0 = qk, vcb, n_lat // ctx_len

        attn = _attention(qk, vcb, qk_ctx, vcb_ctx, ctx_blk0, batch, seq, ctx_len, d_q, d_conv, ctx_queries=False)
        attn_ctx = attn if last else _attention_ctx(qk, vcb, ctx_blk0, batch, ctx_len, d_q, d_conv)

        rows_m = _Rows(n_main, tm_merge, batch, seq, n_lat)
        y = _merge(rows_m, attn, attn_ctx, vcb, z, gates, conv_w, w_attn_b, w_conv_b, l, seq, ctx_len)

        rows_s = _Rows(n_main, tm_small, batch, seq, n_lat)
        x1, vp, meta, counts = _mix_router(rows_s, y, w_mix_b, l, state, g1, ln1_g[l], ln1_b[l],
                                           sc2, sh2, router_w[l], router_b[l], alpha)
        dest, block_e, n_used, pad_lo, pad_hi, n_slots = _slot_layout(meta, counts, n_main)
        xg = _dispatch(vp, dest, pad_lo, pad_hi, n_slots, tm_small, n_chunks)
        y_slots = _experts(xg, block_e, n_used, w_up_b, b_up, w_down_b, b_down, l)
        state = _combine(rows_s, dest, meta, x1, g2, ln2_g[l], ln2_b[l], y_slots, alpha)

    return state[:n_lat].reshape(batch, seq, d)
```
